```python
import math
import jax, jax.numpy as jnp
from jax import lax
import numpy as np

D_MODEL = 2048
BATCH = 2
SEQ = 4096
DEPTH = 1
DEC_BATCH = 4
DEC_SEQ = 2048
PAST_LEN = 128

HEAD_DIM = 128
MIX_WIDTH = D_MODEL
N_ATTN_HEADS = 8
N_KV_HEADS = 2
GQA_GROUP = N_ATTN_HEADS // N_KV_HEADS
ATTN_WIDTH = N_ATTN_HEADS * HEAD_DIM
KV_WIDTH = N_KV_HEADS * HEAD_DIM
N_GMLP_HEADS = 8
GMLP_WIDTH = N_GMLP_HEADS * HEAD_DIM
IN_WIDTH = ATTN_WIDTH + 2 * KV_WIDTH + 2 * GMLP_WIDTH
WINDOW = 128
BLOCK = 128
CHUNK = 128
ROPE_THETA = 500000.0
ROT_DIM = HEAD_DIM // 4
D_FF = 4 * D_MODEL
EPS = 1e-6
NEG_INF = -1e30

kernel_name = "hymba_style_window_gqa_gmlp_encoder"


def rms_norm(x, g):
    xf = x.astype(jnp.float32)
    y = xf * lax.rsqrt(jnp.mean(xf * xf, axis=-1, keepdims=True) + EPS)
    return (y * g.astype(jnp.float32)).astype(x.dtype)


def layer_norm(x, g, b):
    xf = x.astype(jnp.float32)
    mu = jnp.mean(xf, axis=-1, keepdims=True)
    xc = xf - mu
    y = xc * lax.rsqrt(jnp.mean(xc * xc, axis=-1, keepdims=True) + EPS)
    return (y * g.astype(jnp.float32) + b.astype(jnp.float32)).astype(x.dtype)


def partial_rope(x):
    S = x.shape[1]
    pos = jnp.arange(S, dtype=jnp.float32)
    inv_freq = ROPE_THETA ** (-jnp.arange(0, ROT_DIM, 2, dtype=jnp.float32) / ROT_DIM)
    ang = pos[:, None] * inv_freq[None, :]
    cos = jnp.cos(ang)[None, :, None, :]
    sin = jnp.sin(ang)[None, :, None, :]
    xf = x.astype(jnp.float32)
    half = ROT_DIM // 2
    x1 = xf[..., :half]
    x2 = xf[..., half:ROT_DIM]
    out = jnp.concatenate([x1 * cos - x2 * sin, x2 * cos + x1 * sin, xf[..., ROT_DIM:]], axis=-1)
    return out.astype(x.dtype)


def windowed_gqa_sink(q, k, v, sink):
    B, S = q.shape[0], q.shape[1]
    nb = S // BLOCK
    qb = q.reshape(B, nb, BLOCK, N_KV_HEADS, GQA_GROUP, HEAD_DIM)
    pad = ((0, 0), (BLOCK, BLOCK), (0, 0), (0, 0))
    kp = jnp.pad(k, pad).reshape(B, nb + 2, BLOCK, N_KV_HEADS, HEAD_DIM)
    vp = jnp.pad(v, pad).reshape(B, nb + 2, BLOCK, N_KV_HEADS, HEAD_DIM)
    kwin = jnp.concatenate([kp[:, :-2], kp[:, 1:-1], kp[:, 2:]], axis=2)
    vwin = jnp.concatenate([vp[:, :-2], vp[:, 1:-1], vp[:, 2:]], axis=2)
    scale = 1.0 / math.sqrt(HEAD_DIM)
    s = jnp.einsum('bnqkgd,bnmkd->bnkgqm', qb, kwin,
                   preferred_element_type=jnp.float32) * scale
    blk = jnp.arange(nb)[:, None]
    qpos = blk * BLOCK + jnp.arange(BLOCK)[None, :]
    kpos = (blk - 1) * BLOCK + jnp.arange(3 * BLOCK)[None, :]
    valid = ((jnp.abs(qpos[:, :, None] - kpos[:, None, :]) <= WINDOW)
             & (kpos >= 0)[:, None, :] & (kpos < S)[:, None, :])
    s = jnp.where(valid[None, :, None, None], s, NEG_INF)
    sink_l = sink.astype(jnp.float32).reshape(N_KV_HEADS, GQA_GROUP)[None, None, :, :, None, None]
    m = jnp.maximum(jnp.max(s, axis=-1, keepdims=True), sink_l)
    p = jnp.exp(s - m)
    p = p / (jnp.sum(p, axis=-1, keepdims=True) + jnp.exp(sink_l - m))
    o = jnp.einsum('bnkgqm,bnmkd->bnqkgd', p.astype(v.dtype), vwin)
    return o.reshape(B, S, ATTN_WIDTH)


def chunked_gmlp(u, vg, g_ln, b_ln, w_sp, b_sp):
    B, S = u.shape[0], u.shape[1]
    nc = S // CHUNK
    u = jax.nn.gelu(u)
    vg = layer_norm(jax.nn.gelu(vg), g_ln, b_ln)
    vc = vg.reshape(B, nc, CHUNK, N_GMLP_HEADS, HEAD_DIM)
    mixed = jnp.einsum('hij,bnjhc->bnihc', w_sp, vc) + b_sp.T[None, None, :, :, None]
    return u * mixed.reshape(B, S, GMLP_WIDTH)


def encoder_layer(x, g_mix, w_in, g_q, g_k, sink, g_v_ln, b_v_ln, w_spatial, b_spatial,
                  g_attn_out, g_gmlp_out, w_out, g_ffn, w_up, w_down):
    B, S, _ = x.shape
    h = rms_norm(x, g_mix)
    proj = h @ w_in
    o1 = ATTN_WIDTH
    o2 = o1 + KV_WIDTH
    o3 = o2 + KV_WIDTH
    o4 = o3 + GMLP_WIDTH
    q = proj[..., :o1].reshape(B, S, N_ATTN_HEADS, HEAD_DIM)
    k = proj[..., o1:o2].reshape(B, S, N_KV_HEADS, HEAD_DIM)
    v = proj[..., o2:o3].reshape(B, S, N_KV_HEADS, HEAD_DIM)
    u = proj[..., o3:o4]
    vg = proj[..., o4:]
    q = partial_rope(rms_norm(q, g_q))
    k = partial_rope(rms_norm(k, g_k))
    attn = windowed_gqa_sink(q, k, v, sink)
    gm = chunked_gmlp(u, vg, g_v_ln, b_v_ln, w_spatial, b_spatial)
    mix = jnp.concatenate([rms_norm(attn, g_attn_out), rms_norm(gm, g_gmlp_out)], axis=-1)
    x = x + mix @ w_out
    h2 = rms_norm(x, g_ffn)
    x = x + jnp.square(jax.nn.relu(h2 @ w_up)) @ w_down
    return x


def setup_inputs(seed: int = 0) -> dict:
    key = jax.random.key(seed)
    ks = jax.random.split(key, 20)
    f32 = jnp.float32
    nrm = lambda k, shape, s: jax.random.normal(k, shape, f32) * s
    L = DEPTH
    return {
        "x_prompt": jax.random.normal(ks[0], (BATCH, SEQ, D_MODEL), f32),
        "x_sample": jax.random.normal(ks[1], (DEC_BATCH, DEC_SEQ, D_MODEL), f32),
        "g_mix": 1.0 + nrm(ks[2], (L, D_MODEL), 0.02),
        "w_in": nrm(ks[3], (L, D_MODEL, IN_WIDTH), D_MODEL ** -0.5),
        "g_q": 1.0 + nrm(ks[4], (L, HEAD_DIM), 0.02),
        "g_k": 1.0 + nrm(ks[5], (L, HEAD_DIM), 0.02),
        "sink": nrm(ks[6], (L, N_ATTN_HEADS), 0.5),
        "g_v_ln": 1.0 + nrm(ks[7], (L, GMLP_WIDTH), 0.02),
        "b_v_ln": nrm(ks[8], (L, GMLP_WIDTH), 0.02),
        "w_spatial": nrm(ks[9], (L, N_GMLP_HEADS, CHUNK, CHUNK), CHUNK ** -0.5),
        "b_spatial": 1.0 + nrm(ks[10], (L, N_GMLP_HEADS, CHUNK), 0.1),
        "g_attn_out": 1.0 + nrm(ks[11], (L, ATTN_WIDTH), 0.02),
        "g_gmlp_out": 1.0 + nrm(ks[12], (L, GMLP_WIDTH), 0.02),
        "w_out": nrm(ks[13], (L, MIX_WIDTH, D_MODEL), MIX_WIDTH ** -0.5),
        "g_ffn": 1.0 + nrm(ks[14], (L, D_MODEL), 0.02),
        "w_up": nrm(ks[15], (L, D_MODEL, D_FF), D_MODEL ** -0.5),
        "w_down": nrm(ks[16], (L, D_FF, D_MODEL), D_FF ** -0.5),
    }


def reference(x_prompt, x_sample, g_mix, w_in, g_q, g_k, sink, g_v_ln, b_v_ln, w_spatial,
              b_spatial, g_attn_out, g_gmlp_out, w_out, g_ffn, w_up, w_down):
    y_prompt = x_prompt
    y_sample = x_sample
    for l in range(DEPTH):
        params = (g_mix[l], w_in[l], g_q[l], g_k[l], sink[l], g_v_ln[l], b_v_ln[l],
                  w_spatial[l], b_spatial[l], g_attn_out[l], g_gmlp_out[l], w_out[l],
                  g_ffn[l], w_up[l], w_down[l])
        y_prompt = encoder_layer(y_prompt, *params)
        y_sample = encoder_layer(y_sample, *params)
    return (y_prompt, y_sample)
```

```python
import functools
import math

import jax
import jax.numpy as jnp
from jax import lax
from jax.experimental import pallas as pl
from jax.experimental.pallas import tpu as pltpu

D_MODEL = 2048
HEAD_DIM = 128
N_ATTN_HEADS = 8
N_KV_HEADS = 2
GQA_GROUP = N_ATTN_HEADS // N_KV_HEADS
ATTN_WIDTH = N_ATTN_HEADS * HEAD_DIM
KV_WIDTH = N_KV_HEADS * HEAD_DIM
N_GMLP_HEADS = 8
GMLP_WIDTH = N_GMLP_HEADS * HEAD_DIM
IN_WIDTH = ATTN_WIDTH + 2 * KV_WIDTH + 2 * GMLP_WIDTH
BLOCK = 128
ROPE_THETA = 500000.0
ROT_DIM = HEAD_DIM // 4
D_FF = 4 * D_MODEL
EPS = 1e-6
NEG_INF = -1e30

Q_OFF = 0
KV_OFF = ATTN_WIDTH
U_OFF = KV_OFF + 2 * KV_WIDTH
VG_OFF = U_OFF + GMLP_WIDTH

TM_PROJ = 512
TM_MIX = 512
TM_FFN = 1024
TF_FFN = 512

V7X_VMEM_LIMIT_BYTES = 56 * 1024 * 1024

F32 = jnp.float32
BF16 = jnp.bfloat16


def _rms_scale(x):
    return lax.rsqrt(jnp.mean(x * x, axis=-1, keepdims=True) + EPS)


def _const_spec(shape):
    nd = len(shape)
    return pl.BlockSpec(shape, lambda *_: (0,) * nd, pipeline_mode=pl.Buffered(1))


def _in_proj_kernel(x_ref, gmix_ref, w_ref, gq_ref, gk_ref, gln_ref, bln_ref,
                    cos_ref, sina_ref, sinb_ref, q_ref, kv_ref, u_ref, vg_ref):
    x = x_ref[...]
    h = (x * _rms_scale(x) * gmix_ref[...]).astype(BF16)

    cos = cos_ref[...]
    sina = sina_ref[...]
    sinb = sinb_ref[...]

    def norm_rope(xh, g):
        y = xh * _rms_scale(xh) * g
        return (y * cos
                + pltpu.roll(y, HEAD_DIM - ROT_DIM // 2, 1) * sina
                + pltpu.roll(y, ROT_DIM // 2, 1) * sinb)

    def proj(col, width):
        return jnp.dot(h, w_ref[:, col:col + width], preferred_element_type=F32)

    gq = gq_ref[...]
    gk = gk_ref[...]
    half = ATTN_WIDTH // 2
    for c in range(2):
        p = proj(Q_OFF + c * half, half)
        for j in range(half // HEAD_DIM):
            lo = j * HEAD_DIM
            q_ref[:, c * half + lo:c * half + lo + HEAD_DIM] = norm_rope(
                p[:, lo:lo + HEAD_DIM], gq).astype(BF16)

    p = proj(KV_OFF, 2 * KV_WIDTH)
    for j in range(N_KV_HEADS):
        lo = j * HEAD_DIM
        kv_ref[:, lo:lo + HEAD_DIM] = norm_rope(p[:, lo:lo + HEAD_DIM], gk).astype(BF16)
    kv_ref[:, KV_WIDTH:] = p[:, KV_WIDTH:].astype(BF16)

    half = GMLP_WIDTH // 2
    for c in range(2):
        p = proj(U_OFF + c * half, half)
        u_ref[:, c * half:(c + 1) * half] = jax.nn.gelu(p).astype(BF16)

    p = jax.nn.gelu(proj(VG_OFF, GMLP_WIDTH))
    mu = jnp.mean(p, axis=-1, keepdims=True)
    pc = p - mu
    y = pc * lax.rsqrt(jnp.mean(pc * pc, axis=-1, keepdims=True) + EPS)
    vg_ref[...] = (y * gln_ref[...] + bln_ref[...]).astype(BF16)


def _in_proj(x, g_mix, w_in, g_q, g_k, g_ln, b_ln, rope, seq):
    t = x.shape[0]
    tm = TM_PROJ
    tiles_per_seq = seq // tm
    row = lambda i: (i, 0)
    rope_spec = pl.BlockSpec((tm, HEAD_DIM), lambda i: (i % tiles_per_seq, 0))
    return pl.pallas_call(
        _in_proj_kernel,
        grid=(t // tm,),
        in_specs=[
            pl.BlockSpec((tm, D_MODEL), row),
            _const_spec((1, D_MODEL)),
            _const_spec((D_MODEL, IN_WIDTH)),
            _const_spec((1, HEAD_DIM)),
            _const_spec((1, HEAD_DIM)),
            _const_spec((1, GMLP_WIDTH)),
            _const_spec((1, GMLP_WIDTH)),
            rope_spec, rope_spec, rope_spec,
        ],
        out_specs=[
            pl.BlockSpec((tm, ATTN_WIDTH), row),
            pl.BlockSpec((tm, 2 * KV_WIDTH), row),
            pl.BlockSpec((tm, GMLP_WIDTH), row),
            pl.BlockSpec((tm, GMLP_WIDTH), row),
        ],
        out_shape=[
            jax.ShapeDtypeStruct((t, ATTN_WIDTH), BF16),
            jax.ShapeDtypeStruct((t, 2 * KV_WIDTH), BF16),
            jax.ShapeDtypeStruct((t, GMLP_WIDTH), BF16),
            jax.ShapeDtypeStruct((t, GMLP_WIDTH), BF16),
        ],
        compiler_params=pltpu.CompilerParams(
            dimension_semantics=("arbitrary",),
            vmem_limit_bytes=V7X_VMEM_LIMIT_BYTES),
        name="in_proj",
    )(x, g_mix, w_in, g_q, g_k, g_ln, b_ln, *rope)


def _mixer_kernel(tiles_per_seq, x_ref, q_ref, kvp_ref, kvc_ref, kvn_ref, u_ref, vg_ref,
                  sink_ref, wsp_ref, bsp_ref, gao_ref, ggo_ref, wout_ref, o_ref,
                  kv_buf, mix_buf):
    tm = x_ref.shape[0]
    nblk = tm // BLOCK
    tile_in_seq = pl.program_id(0) % tiles_per_seq
    is_first = tile_in_seq == 0
    is_last = tile_in_seq == tiles_per_seq - 1

    kv_buf[0:BLOCK, :] = kvp_ref[...]
    kv_buf[BLOCK:BLOCK + tm, :] = kvc_ref[...]
    kv_buf[BLOCK + tm:, :] = kvn_ref[...]

    qi = lax.broadcasted_iota(jnp.int32, (BLOCK, BLOCK), 0)
    kj = lax.broadcasted_iota(jnp.int32, (BLOCK, BLOCK), 1)
    zero = jnp.zeros((BLOCK, BLOCK), F32)
    prev_bias = jnp.where(kj >= qi, 0.0, NEG_INF).astype(F32)
    next_bias = jnp.where(kj <= qi, 0.0, NEG_INF).astype(F32)
    scale = 1.0 / math.sqrt(HEAD_DIM)

    for b in range(nblk):
        pb = prev_bias
        nb_ = next_bias
        if b == 0:
            pb = jnp.where(is_first, NEG_INF, pb)
        if b == nblk - 1:
            nb_ = jnp.where(is_last, NEG_INF, nb_)
        bias = jnp.concatenate([pb, zero, nb_], axis=1)[None]
        r0 = b * BLOCK
        for kvh in range(N_KV_HEADS):
            qs = jnp.concatenate(
                [q_ref[r0:r0 + BLOCK, (kvh * GQA_GROUP + g) * HEAD_DIM:
                       (kvh * GQA_GROUP + g + 1) * HEAD_DIM] for g in range(GQA_GROUP)],
                axis=0)
            kwin = kv_buf[r0:r0 + 3 * BLOCK, kvh * HEAD_DIM:(kvh + 1) * HEAD_DIM]
            vwin = kv_buf[r0:r0 + 3 * BLOCK,
                          KV_WIDTH + kvh * HEAD_DIM:KV_WIDTH + (kvh + 1) * HEAD_DIM]
            s = lax.dot_general(qs, kwin, (((1,), (1,)), ((), ())),
                                preferred_element_type=F32) * scale
            s = (s.reshape(GQA_GROUP, BLOCK, 3 * BLOCK) + bias)
            sink = jnp.concatenate(
                [jnp.full((1, 1, 1), sink_ref[kvh * GQA_GROUP + g], F32)
                 for g in range(GQA_GROUP)], axis=0)
            m = jnp.maximum(jnp.max(s, axis=-1, keepdims=True), sink)
            p = jnp.exp(s - m)
            denom = jnp.sum(p, axis=-1, keepdims=True) + jnp.exp(sink - m)
            o = jnp.dot(p.reshape(GQA_GROUP * BLOCK, 3 * BLOCK).astype(BF16), vwin,
                        preferred_element_type=F32)
            o = o.reshape(GQA_GROUP, BLOCK, HEAD_DIM) * (1.0 / denom)
            for g in range(GQA_GROUP):
                c0 = (kvh * GQA_GROUP + g) * HEAD_DIM
                mix_buf[r0:r0 + BLOCK, c0:c0 + HEAD_DIM] = o[g]

    for hd in range(N_GMLP_HEADS):
        c0 = hd * HEAD_DIM
        vh = jnp.concatenate(
            [vg_ref[c * BLOCK:(c + 1) * BLOCK, c0:c0 + HEAD_DIM] for c in range(nblk)], axis=1)
        mixed = jnp.dot(wsp_ref[hd], vh, preferred_element_type=F32)
        bias_h = bsp_ref[hd]
        for c in range(nblk):
            uu = u_ref[c * BLOCK:(c + 1) * BLOCK, c0:c0 + HEAD_DIM].astype(F32)
            mix_buf[c * BLOCK:(c + 1) * BLOCK, ATTN_WIDTH + c0:ATTN_WIDTH + c0 + HEAD_DIM] = (
                uu * (mixed[:, c * BLOCK:(c + 1) * BLOCK] + bias_h))

    attn = mix_buf[:, :ATTN_WIDTH]
    gm = mix_buf[:, ATTN_WIDTH:]
    mix = jnp.concatenate(
        [(attn * _rms_scale(attn) * gao_ref[...]).astype(BF16),
         (gm * _rms_scale(gm) * ggo_ref[...]).astype(BF16)], axis=1)
    o_ref[...] = x_ref[...] + jnp.dot(mix, wout_ref[...], preferred_element_type=F32)


def _mixer(x, q, kv, u, vg, sink, w_sp, b_sp, g_ao, g_go, w_out, seq):
    t = x.shape[0]
    tm = TM_MIX
    nblk = tm // BLOCK
    tiles_per_seq = seq // tm
    last_block = t // BLOCK - 1
    row = lambda i: (i, 0)
    return pl.pallas_call(
        functools.partial(_mixer_kernel, tiles_per_seq),
        grid=(t // tm,),
        in_specs=[
            pl.BlockSpec((tm, D_MODEL), row),
            pl.BlockSpec((tm, ATTN_WIDTH), row),
            pl.BlockSpec((BLOCK, 2 * KV_WIDTH), lambda i: (jnp.maximum(i * nblk - 1, 0), 0)),
            pl.BlockSpec((tm, 2 * KV_WIDTH), row),
            pl.BlockSpec((BLOCK, 2 * KV_WIDTH),
                         lambda i: (jnp.minimum((i + 1) * nblk, last_block), 0)),
            pl.BlockSpec((tm, GMLP_WIDTH), row),
            pl.BlockSpec((tm, GMLP_WIDTH), row),
            pl.BlockSpec(memory_space=pltpu.SMEM),
            _const_spec((N_GMLP_HEADS, BLOCK, BLOCK)),
            _const_spec((N_GMLP_HEADS, BLOCK, BLOCK)),
            _const_spec((1, ATTN_WIDTH)),
            _const_spec((1, GMLP_WIDTH)),
            _const_spec((D_MODEL, D_MODEL)),
        ],
        out_specs=pl.BlockSpec((tm, D_MODEL), row),
        out_shape=jax.ShapeDtypeStruct((t, D_MODEL), F32),
        scratch_shapes=[
            pltpu.VMEM((tm + 2 * BLOCK, 2 * KV_WIDTH), BF16),
            pltpu.VMEM((tm, D_MODEL), F32),
        ],
        compiler_params=pltpu.CompilerParams(
            dimension_semantics=("arbitrary",),
            vmem_limit_bytes=V7X_VMEM_LIMIT_BYTES),
        name="mixer",
    )(x, q, kv, kv, kv, u, vg, sink, w_sp, b_sp, g_ao, g_go, w_out)


def _ffn_kernel(x_ref, g_ref, wup_ref, wdn_ref, o_ref, h_ref):
    @pl.when(pl.program_id(1) == 0)
    def _():
        x = x_ref[...]
        h_ref[...] = (x * _rms_scale(x) * g_ref[...]).astype(BF16)
        o_ref[...] = x

    up = jnp.dot(h_ref[...], wup_ref[...], preferred_element_type=F32)
    act = jnp.square(jnp.maximum(up, 0.0)).astype(BF16)
    o_ref[...] += jnp.dot(act, wdn_ref[...], preferred_element_type=F32)


def _ffn(x, g_ffn, w_up, w_down):
    t = x.shape[0]
    tm, tf = TM_FFN, TF_FFN
    return pl.pallas_call(
        _ffn_kernel,
        grid=(t // tm, D_FF // tf),
        in_specs=[
            pl.BlockSpec((tm, D_MODEL), lambda i, f: (i, 0)),
            _const_spec((1, D_MODEL)),
            pl.BlockSpec((D_MODEL, tf), lambda i, f: (0, f)),
            pl.BlockSpec((tf, D_MODEL), lambda i, f: (f, 0)),
        ],
        out_specs=pl.BlockSpec((tm, D_MODEL), lambda i, f: (i, 0)),
        out_shape=jax.ShapeDtypeStruct((t, D_MODEL), F32),
        scratch_shapes=[pltpu.VMEM((tm, D_MODEL), BF16)],
        compiler_params=pltpu.CompilerParams(
            dimension_semantics=("arbitrary", "arbitrary"),
            vmem_limit_bytes=V7X_VMEM_LIMIT_BYTES),
        name="ffn",
    )(x, g_ffn, w_up, w_down)


def _rope_tables(seq):
    pos = jnp.arange(seq, dtype=F32)
    inv_freq = ROPE_THETA ** (-jnp.arange(0, ROT_DIM, 2, dtype=F32) / ROT_DIM)
    ang = pos[:, None] * inv_freq[None, :]
    cos, sin = jnp.cos(ang), jnp.sin(ang)
    half = ROT_DIM // 2
    pad = HEAD_DIM - ROT_DIM
    cos_t = jnp.concatenate([cos, cos, jnp.ones((seq, pad), F32)], axis=1)
    sina_t = jnp.concatenate([-sin, jnp.zeros((seq, HEAD_DIM - half), F32)], axis=1)
    sinb_t = jnp.concatenate([jnp.zeros((seq, half), F32), sin, jnp.zeros((seq, pad), F32)], axis=1)
    return cos_t, sina_t, sinb_t


def _encoder_layer(x, p):
    bsz, seq, _ = x.shape
    xt = x.reshape(bsz * seq, D_MODEL)
    q, kv, u, vg = _in_proj(xt, p["g_mix"], p["w_in"], p["g_q"], p["g_k"], p["g_v_ln"],
                            p["b_v_ln"], _rope_tables(seq), seq)
    x1 = _mixer(xt, q, kv, u, vg, p["sink"], p["w_spatial"], p["b_spatial"],
                p["g_attn_out"], p["g_gmlp_out"], p["w_out"], seq)
    y = _ffn(x1, p["g_ffn"], p["w_up"], p["w_down"])
    return y.reshape(bsz, seq, D_MODEL)


def kernel(x_prompt, x_sample, g_mix, w_in, g_q, g_k, sink, g_v_ln, b_v_ln, w_spatial, b_spatial, g_attn_out, g_gmlp_out, w_out, g_ffn, w_up, w_down):
    y_prompt, y_sample = x_prompt, x_sample
    for l in range(g_mix.shape[0]):
        p = {
            "g_mix": g_mix[l][None], "w_in": w_in[l].astype(BF16),
            "g_q": g_q[l][None], "g_k": g_k[l][None], "sink": sink[l],
            "g_v_ln": g_v_ln[l][None], "b_v_ln": b_v_ln[l][None],
            "w_spatial": w_spatial[l].astype(BF16),
            "b_spatial": jnp.broadcast_to(b_spatial[l][:, :, None],
                                          (N_GMLP_HEADS, BLOCK, BLOCK)),
            "g_attn_out": g_attn_out[l][None], "g_gmlp_out": g_gmlp_out[l][None],
            "w_out": w_out[l].astype(BF16), "g_ffn": g_ffn[l][None],
            "w_up": w_up[l].astype(BF16), "w_down": w_down[l].astype(BF16),
        }
        y_prompt = _encoder_layer(y_prompt, p)
        y_sample = _encoder_layer(y_sample, p)
    return (y_prompt, y_sample)
```

```python
import functools
import math

import jax
import jax.numpy as jnp
from jax import lax
from jax.experimental import pallas as pl
from jax.experimental.pallas import tpu as pltpu

D_MODEL = 2048
HEAD_DIM = 128
N_ATTN_HEADS = 8
N_KV_HEADS = 2
GQA_GROUP = N_ATTN_HEADS // N_KV_HEADS
ATTN_WIDTH = N_ATTN_HEADS * HEAD_DIM
KV_WIDTH = N_KV_HEADS * HEAD_DIM
N_GMLP_HEADS = 8
GMLP_WIDTH = N_GMLP_HEADS * HEAD_DIM
IN_WIDTH = ATTN_WIDTH + 2 * KV_WIDTH + 2 * GMLP_WIDTH
BLOCK = 128
ROPE_THETA = 500000.0
ROT_DIM = HEAD_DIM // 4
D_FF = 4 * D_MODEL
EPS = 1e-6
NEG_INF = -1e30

Q_OFF = 0
KV_OFF = ATTN_WIDTH
U_OFF = KV_OFF + 2 * KV_WIDTH
VG_OFF = U_OFF + GMLP_WIDTH

TM_PROJ = 512
TM_MIX = 512
TM_FFN = 1024
TF_FFN = 512

V7X_VMEM_LIMIT_BYTES = 56 * 1024 * 1024

F32 = jnp.float32
BF16 = jnp.bfloat16


def _rms_scale(x):
    return lax.rsqrt(jnp.mean(x * x, axis=-1, keepdims=True) + EPS)


def _const_spec(shape):
    nd = len(shape)
    return pl.BlockSpec(shape, lambda *_: (0,) * nd, pipeline_mode=pl.Buffered(1))


def _in_proj_kernel(x_ref, gmix_ref, w_ref, gq_ref, gk_ref, gln_ref, bln_ref,
                    cos_ref, sina_ref, sinb_ref, q_ref, kv_ref, u_ref, vg_ref):
    x = x_ref[...]
    h = (x * _rms_scale(x) * gmix_ref[...]).astype(BF16)

    cos = cos_ref[...]
    sina = sina_ref[...]
    sinb = sinb_ref[...]

    def norm_rope(xh, g):
        y = xh * _rms_scale(xh) * g
        return (y * cos
                + pltpu.roll(y, HEAD_DIM - ROT_DIM // 2, 1) * sina
                + pltpu.roll(y, ROT_DIM // 2, 1) * sinb)

    def proj(col, width):
        return jnp.dot(h, w_ref[:, col:col + width], preferred_element_type=F32)

    gq = gq_ref[...]
    gk = gk_ref[...]
    half = ATTN_WIDTH // 2
    for c in range(2):
        p = proj(Q_OFF + c * half, half)
        for j in range(half // HEAD_DIM):
            lo = j * HEAD_DIM
            q_ref[:, c * half + lo:c * half + lo + HEAD_DIM] = norm_rope(
                p[:, lo:lo + HEAD_DIM], gq).astype(BF16)

    p = proj(KV_OFF, 2 * KV_WIDTH)
    for j in range(N_KV_HEADS):
        lo = j * HEAD_DIM
        kv_ref[:, lo:lo + HEAD_DIM] = norm_rope(p[:, lo:lo + HEAD_DIM], gk).astype(BF16)
    kv_ref[:, KV_WIDTH:] = p[:, KV_WIDTH:].astype(BF16)

    half = GMLP_WIDTH // 2
    for c in range(2):
        p = proj(U_OFF + c * half, half)
        u_ref[:, c * half:(c + 1) * half] = jax.nn.gelu(p).astype(BF16)

    p = jax.nn.gelu(proj(VG_OFF, GMLP_WIDTH))
    mu = jnp.mean(p, axis=-1, keepdims=True)
    pc = p - mu
    y = pc * lax.rsqrt(jnp.mean(pc * pc, axis=-1, keepdims=True) + EPS)
    vg_ref[...] = (y * gln_ref[...] + bln_ref[...]).astype(BF16)


def _in_proj(x, g_mix, w_in, g_q, g_k, g_ln, b_ln, rope, seq):
    t = x.shape[0]
    tm = TM_PROJ
    tiles_per_seq = seq // tm
    row = lambda i: (i, 0)
    rope_spec = pl.BlockSpec((tm, HEAD_DIM), lambda i: (i % tiles_per_seq, 0))
    return pl.pallas_call(
        _in_proj_kernel,
        grid=(t // tm,),
        in_specs=[
            pl.BlockSpec((tm, D_MODEL), row),
            _const_spec((1, D_MODEL)),
            _const_spec((D_MODEL, IN_WIDTH)),
            _const_spec((1, HEAD_DIM)),
            _const_spec((1, HEAD_DIM)),
            _const_spec((1, GMLP_WIDTH)),
            _const_spec((1, GMLP_WIDTH)),
            rope_spec, rope_spec, rope_spec,
        ],
        out_specs=[
            pl.BlockSpec((tm, ATTN_WIDTH), row),
            pl.BlockSpec((tm, 2 * KV_WIDTH), row),
            pl.BlockSpec((tm, GMLP_WIDTH), row),
            pl.BlockSpec((tm, GMLP_WIDTH), row),
        ],
        out_shape=[
            jax.ShapeDtypeStruct((t, ATTN_WIDTH), BF16),
            jax.ShapeDtypeStruct((t, 2 * KV_WIDTH), BF16),
            jax.ShapeDtypeStruct((t, GMLP_WIDTH), BF16),
            jax.ShapeDtypeStruct((t, GMLP_WIDTH), BF16),
        ],
        compiler_params=pltpu.CompilerParams(
            dimension_semantics=("arbitrary",),
            vmem_limit_bytes=V7X_VMEM_LIMIT_BYTES),
        name="in_proj",
    )(x, g_mix, w_in, g_q, g_k, g_ln, b_ln, *rope)


def _mixer_kernel(tiles_per_seq, n_tiles, x_ref, q_ref, kvp_ref, kvc_ref, kvn_ref, u_ref,
                  vg_ref, sink_ref, wsp_ref, bsp_ref, gao_ref, ggo_ref, wout_ref, o_ref,
                  kv_buf, mix_buf, mixn_buf):
    tm = x_ref.shape[0]
    nblk = tm // BLOCK
    step = pl.program_id(0)
    tile_in_seq = jnp.minimum(step, n_tiles - 1) % tiles_per_seq
    is_first = tile_in_seq == 0
    is_last = tile_in_seq == tiles_per_seq - 1
    slot_w = step % 2
    slot_r = 1 - slot_w

    @pl.when(step == 0)
    def _():
        mixn_buf[1] = jnp.zeros(mixn_buf.shape[1:], BF16)

    kv_buf[0:BLOCK, :] = kvp_ref[...]
    kv_buf[BLOCK:BLOCK + tm, :] = kvc_ref[...]
    kv_buf[BLOCK + tm:, :] = kvn_ref[...]

    qi = lax.broadcasted_iota(jnp.int32, (BLOCK, BLOCK), 0)
    kj = lax.broadcasted_iota(jnp.int32, (BLOCK, BLOCK), 1)
    zero = jnp.zeros((BLOCK, BLOCK), F32)
    prev_bias = jnp.where(kj >= qi, 0.0, NEG_INF).astype(F32)
    next_bias = jnp.where(kj <= qi, 0.0, NEG_INF).astype(F32)
    scale = 1.0 / math.sqrt(HEAD_DIM)

    def band_bias(b):
        pb, nb_ = prev_bias, next_bias
        if b == 0:
            pb = jnp.where(is_first, NEG_INF, pb)
        if b == nblk - 1:
            nb_ = jnp.where(is_last, NEG_INF, nb_)
        return jnp.concatenate([pb, zero, nb_], axis=1)[None]

    def scores(b, kvh):
        r0 = b * BLOCK
        qs = jnp.concatenate(
            [q_ref[r0:r0 + BLOCK, (kvh * GQA_GROUP + g) * HEAD_DIM:
                   (kvh * GQA_GROUP + g + 1) * HEAD_DIM] for g in range(GQA_GROUP)],
            axis=0)
        kwin = kv_buf[r0:r0 + 3 * BLOCK, kvh * HEAD_DIM:(kvh + 1) * HEAD_DIM]
        return lax.dot_general(qs, kwin, (((1,), (1,)), ((), ())),
                               preferred_element_type=F32)

    def softmax_pv(b, kvh, s):
        r0 = b * BLOCK
        vwin = kv_buf[r0:r0 + 3 * BLOCK,
                      KV_WIDTH + kvh * HEAD_DIM:KV_WIDTH + (kvh + 1) * HEAD_DIM]
        s = (s * scale).reshape(GQA_GROUP, BLOCK, 3 * BLOCK) + band_bias(b)
        sink = jnp.concatenate(
            [jnp.full((1, 1, 1), sink_ref[kvh * GQA_GROUP + g], F32)
             for g in range(GQA_GROUP)], axis=0)
        m = jnp.maximum(jnp.max(s, axis=-1, keepdims=True), sink)
        p = jnp.exp(s - m)
        denom = jnp.sum(p, axis=-1, keepdims=True) + jnp.exp(sink - m)
        o = jnp.dot(p.reshape(GQA_GROUP * BLOCK, 3 * BLOCK).astype(BF16), vwin,
                    preferred_element_type=F32)
        o = o.reshape(GQA_GROUP, BLOCK, HEAD_DIM) * (1.0 / denom)
        for g in range(GQA_GROUP):
            c0 = (kvh * GQA_GROUP + g) * HEAD_DIM
            mix_buf[r0:r0 + BLOCK, c0:c0 + HEAD_DIM] = o[g]

    def spatial_gate(hd):
        c0 = hd * HEAD_DIM
        vh = jnp.concatenate(
            [vg_ref[c * BLOCK:(c + 1) * BLOCK, c0:c0 + HEAD_DIM] for c in range(nblk)], axis=1)
        mixed = jnp.dot(wsp_ref[hd], vh, preferred_element_type=F32)
        bias_h = bsp_ref[hd]
        for c in range(nblk):
            uu = u_ref[c * BLOCK:(c + 1) * BLOCK, c0:c0 + HEAD_DIM].astype(F32)
            mix_buf[c * BLOCK:(c + 1) * BLOCK, ATTN_WIDTH + c0:ATTN_WIDTH + c0 + HEAD_DIM] = (
                uu * (mixed[:, c * BLOCK:(c + 1) * BLOCK] + bias_h))

    chains = [(b, kvh) for b in range(nblk) for kvh in range(N_KV_HEADS)]
    slab = D_MODEL // len(chains)
    mix_prev = mixn_buf.at[slot_r]
    s_next = scores(*chains[0])
    for k, (b, kvh) in enumerate(chains):
        s_cur = s_next
        if k + 1 < len(chains):
            s_next = scores(*chains[k + 1])
        c0 = k * slab
        o_ref[:, c0:c0 + slab] = x_ref[:, c0:c0 + slab] + jnp.dot(
            mix_prev[...], wout_ref[:, c0:c0 + slab], preferred_element_type=F32)
        softmax_pv(b, kvh, s_cur)
        for hd in range(k * N_GMLP_HEADS // len(chains), (k + 1) * N_GMLP_HEADS // len(chains)):
            spatial_gate(hd)

    attn = mix_buf[:, :ATTN_WIDTH]
    gm = mix_buf[:, ATTN_WIDTH:]
    mixn_buf[slot_w, :, :ATTN_WIDTH] = (attn * _rms_scale(attn) * gao_ref[...]).astype(BF16)
    mixn_buf[slot_w, :, ATTN_WIDTH:] = (gm * _rms_scale(gm) * ggo_ref[...]).astype(BF16)


def _mixer(x, q, kv, u, vg, sink, w_sp, b_sp, g_ao, g_go, w_out, seq):
    t = x.shape[0]
    tm = TM_MIX
    nblk = tm // BLOCK
    n_tiles = t // tm
    tiles_per_seq = seq // tm
    last_block = t // BLOCK - 1
    cur = lambda i: jnp.minimum(i, n_tiles - 1)
    row = lambda i: (cur(i), 0)
    lag = lambda i: (jnp.maximum(i - 1, 0), 0)
    return pl.pallas_call(
        functools.partial(_mixer_kernel, tiles_per_seq, n_tiles),
        grid=(n_tiles + 1,),
        in_specs=[
            pl.BlockSpec((tm, D_MODEL), lag),
            pl.BlockSpec((tm, ATTN_WIDTH), row),
            pl.BlockSpec((BLOCK, 2 * KV_WIDTH),
                         lambda i: (jnp.maximum(cur(i) * nblk - 1, 0), 0)),
            pl.BlockSpec((tm, 2 * KV_WIDTH), row),
            pl.BlockSpec((BLOCK, 2 * KV_WIDTH),
                         lambda i: (jnp.minimum((cur(i) + 1) * nblk, last_block), 0)),
            pl.BlockSpec((tm, GMLP_WIDTH), row),
            pl.BlockSpec((tm, GMLP_WIDTH), row),
            pl.BlockSpec(memory_space=pltpu.SMEM),
            _const_spec((N_GMLP_HEADS, BLOCK, BLOCK)),
            _const_spec((N_GMLP_HEADS, BLOCK, BLOCK)),
            _const_spec((1, ATTN_WIDTH)),
            _const_spec((1, GMLP_WIDTH)),
            _const_spec((D_MODEL, D_MODEL)),
        ],
        out_specs=pl.BlockSpec((tm, D_MODEL), lag),
        out_shape=jax.ShapeDtypeStruct((t, D_MODEL), F32),
        scratch_shapes=[
            pltpu.VMEM((tm + 2 * BLOCK, 2 * KV_WIDTH), BF16),
            pltpu.VMEM((tm, D_MODEL), F32),
            pltpu.VMEM((2, tm, D_MODEL), BF16),
        ],
        compiler_params=pltpu.CompilerParams(
            dimension_semantics=("arbitrary",),
            vmem_limit_bytes=V7X_VMEM_LIMIT_BYTES),
        name="mixer",
    )(x, q, kv, kv, kv, u, vg, sink, w_sp, b_sp, g_ao, g_go, w_out)


def _ffn_kernel(x_ref, g_ref, wup_ref, wdn_ref, o_ref, h_ref):
    @pl.when(pl.program_id(1) == 0)
    def _():
        x = x_ref[...]
        h_ref[...] = (x * _rms_scale(x) * g_ref[...]).astype(BF16)
        o_ref[...] = x

    up = jnp.dot(h_ref[...], wup_ref[...], preferred_element_type=F32)
    act = jnp.square(jnp.maximum(up, 0.0)).astype(BF16)
    o_ref[...] += jnp.dot(act, wdn_ref[...], preferred_element_type=F32)


def _ffn(x, g_ffn, w_up, w_down):
    t = x.shape[0]
    tm, tf = TM_FFN, TF_FFN
    return pl.pallas_call(
        _ffn_kernel,
        grid=(t // tm, D_FF // tf),
        in_specs=[
            pl.BlockSpec((tm, D_MODEL), lambda i, f: (i, 0)),
            _const_spec((1, D_MODEL)),
            pl.BlockSpec((D_MODEL, tf), lambda i, f: (0, f)),
            pl.BlockSpec((tf, D_MODEL), lambda i, f: (f, 0)),
        ],
        out_specs=pl.BlockSpec((tm, D_MODEL), lambda i, f: (i, 0)),
        out_shape=jax.ShapeDtypeStruct((t, D_MODEL), F32),
        scratch_shapes=[pltpu.VMEM((tm, D_MODEL), BF16)],
        compiler_params=pltpu.CompilerParams(
            dimension_semantics=("arbitrary", "arbitrary"),
            vmem_limit_bytes=V7X_VMEM_LIMIT_BYTES),
        name="ffn",
    )(x, g_ffn, w_up, w_down)


def _rope_tables(seq):
    pos = jnp.arange(seq, dtype=F32)
    inv_freq = ROPE_THETA ** (-jnp.arange(0, ROT_DIM, 2, dtype=F32) / ROT_DIM)
    ang = pos[:, None] * inv_freq[None, :]
    cos, sin = jnp.cos(ang), jnp.sin(ang)
    half = ROT_DIM // 2
    pad = HEAD_DIM - ROT_DIM
    cos_t = jnp.concatenate([cos, cos, jnp.ones((seq, pad), F32)], axis=1)
    sina_t = jnp.concatenate([-sin, jnp.zeros((seq, HEAD_DIM - half), F32)], axis=1)
    sinb_t = jnp.concatenate([jnp.zeros((seq, half), F32), sin, jnp.zeros((seq, pad), F32)], axis=1)
    return cos_t, sina_t, sinb_t


def _encoder_layer(x, p):
    bsz, seq, _ = x.shape
    xt = x.reshape(bsz * seq, D_MODEL)
    q, kv, u, vg = _in_proj(xt, p["g_mix"], p["w_in"], p["g_q"], p["g_k"], p["g_v_ln"],
                            p["b_v_ln"], _rope_tables(seq), seq)
    x1 = _mixer(xt, q, kv, u, vg, p["sink"], p["w_spatial"], p["b_spatial"],
                p["g_attn_out"], p["g_gmlp_out"], p["w_out"], seq)
    y = _ffn(x1, p["g_ffn"], p["w_up"], p["w_down"])
    return y.reshape(bsz, seq, D_MODEL)


def kernel(x_prompt, x_sample, g_mix, w_in, g_q, g_k, sink, g_v_ln, b_v_ln, w_spatial, b_spatial, g_attn_out, g_gmlp_out, w_out, g_ffn, w_up, w_down):
    y_prompt, y_sample = x_prompt, x_sample
    for l in range(g_mix.shape[0]):
        p = {
            "g_mix": g_mix[l][None], "w_in": w_in[l].astype(BF16),
            "g_q": g_q[l][None], "g_k": g_k[l][None], "sink": sink[l],
            "g_v_ln": g_v_ln[l][None], "b_v_ln": b_v_ln[l][None],
            "w_spatial": w_spatial[l].astype(BF16),
            "b_spatial": jnp.broadcast_to(b_spatial[l][:, :, None],
                                          (N_GMLP_HEADS, BLOCK, BLOCK)),
            "g_attn_out": g_attn_out[l][None], "g_gmlp_out": g_gmlp_out[l][None],
            "w_out": w_out[l].astype(BF16), "g_ffn": g_ffn[l][None],
            "w_up": w_up[l].astype(BF16), "w_down": w_down[l].astype(BF16),
        }
        y_prompt = _encoder_layer(y_prompt, p)
        y_sample = _encoder_layer(y_sample, p)
    return (y_prompt, y_sample)
```

```python
import functools
import math

import jax
import jax.numpy as jnp
from jax import lax
from jax.experimental import pallas as pl
from jax.experimental.pallas import tpu as pltpu

D_MODEL = 2048
HEAD_DIM = 128
N_ATTN_HEADS = 8
N_KV_HEADS = 2
GQA_GROUP = N_ATTN_HEADS // N_KV_HEADS
ATTN_WIDTH = N_ATTN_HEADS * HEAD_DIM
KV_WIDTH = N_KV_HEADS * HEAD_DIM
N_GMLP_HEADS = 8
GMLP_WIDTH = N_GMLP_HEADS * HEAD_DIM
IN_WIDTH = ATTN_WIDTH + 2 * KV_WIDTH + 2 * GMLP_WIDTH
BLOCK = 128
ROPE_THETA = 500000.0
ROT_DIM = HEAD_DIM // 4
D_FF = 4 * D_MODEL
EPS = 1e-6
NEG_INF = -1e30

Q_OFF = 0
KV_OFF = ATTN_WIDTH
U_OFF = KV_OFF + 2 * KV_WIDTH
VG_OFF = U_OFF + GMLP_WIDTH

TM_PROJ = 1024
ROW_PARTS_PROJ = 8
TM_MIX = 512
TM_FFN = 1024
TF_FFN = 512

V7X_VMEM_LIMIT_BYTES = 56 * 1024 * 1024

F32 = jnp.float32
BF16 = jnp.bfloat16


def _rms_scale(x):
    return lax.rsqrt(jnp.mean(x * x, axis=-1, keepdims=True) + EPS)


def _const_spec(shape):
    nd = len(shape)
    return pl.BlockSpec(shape, lambda *_: (0,) * nd, pipeline_mode=pl.Buffered(1))


def _in_proj_kernel(x_ref, gmix_ref, w_ref, gq_ref, gk_ref, gln_ref, bln_ref,
                    cos_ref, sina_ref, sinb_ref, q_ref, kv_ref, u_ref, vg_ref):
    tm = x_ref.shape[0]
    rows_per_part = tm // ROW_PARTS_PROJ

    def normed(r0):
        x = x_ref[r0:r0 + rows_per_part, :]
        return (x * _rms_scale(x) * gmix_ref[...]).astype(BF16)

    def norm_rope(r0, xh, g):
        rows = slice(r0, r0 + rows_per_part)
        y = xh * _rms_scale(xh) * g
        return (y * cos_ref[rows, :]
                + pltpu.roll(y, HEAD_DIM - ROT_DIM // 2, 1) * sina_ref[rows, :]
                + pltpu.roll(y, ROT_DIM // 2, 1) * sinb_ref[rows, :])

    def vg_epilogue(r0, p):
        p = jax.nn.gelu(p)
        mu = jnp.mean(p, axis=-1, keepdims=True)
        pc = p - mu
        y = pc * lax.rsqrt(jnp.mean(pc * pc, axis=-1, keepdims=True) + EPS)
        vg_ref[r0:r0 + rows_per_part, :] = (y * gln_ref[...] + bln_ref[...]).astype(BF16)

    def u_epilogue(c0, r0, p):
        u_ref[r0:r0 + rows_per_part, c0:c0 + p.shape[1]] = jax.nn.gelu(p).astype(BF16)

    def q_epilogue(c0, r0, p):
        g = gq_ref[...]
        for lo in range(0, p.shape[1], HEAD_DIM):
            q_ref[r0:r0 + rows_per_part, c0 + lo:c0 + lo + HEAD_DIM] = norm_rope(
                r0, p[:, lo:lo + HEAD_DIM], g).astype(BF16)

    def kv_epilogue(r0, p):
        g = gk_ref[...]
        for lo in range(0, KV_WIDTH, HEAD_DIM):
            kv_ref[r0:r0 + rows_per_part, lo:lo + HEAD_DIM] = norm_rope(
                r0, p[:, lo:lo + HEAD_DIM], g).astype(BF16)
        kv_ref[r0:r0 + rows_per_part, KV_WIDTH:] = p[:, KV_WIDTH:].astype(BF16)

    half_q, half_u = ATTN_WIDTH // 2, GMLP_WIDTH // 2
    groups = [
        (Q_OFF, half_q, functools.partial(q_epilogue, 0)),
        (Q_OFF + half_q, half_q, functools.partial(q_epilogue, half_q)),
        (KV_OFF, 2 * KV_WIDTH, kv_epilogue),
        (U_OFF, half_u, functools.partial(u_epilogue, 0)),
        (U_OFF + half_u, half_u, functools.partial(u_epilogue, half_u)),
        (VG_OFF, GMLP_WIDTH, vg_epilogue),
    ]
    stages = [(part, g) for part in range(ROW_PARTS_PROJ) for g in range(len(groups))]
    h = {0: normed(0)}

    def proj(stage):
        part, g = stage
        col, width, _ = groups[g]
        return jnp.dot(h[part], w_ref[:, col:col + width], preferred_element_type=F32)

    p_next = proj(stages[0])
    for k, (part, g) in enumerate(stages):
        p_cur = p_next
        if g == 0 and part + 1 < ROW_PARTS_PROJ:
            h[part + 1] = normed((part + 1) * rows_per_part)
        if k + 1 < len(stages):
            p_next = proj(stages[k + 1])
        groups[g][2](part * rows_per_part, p_cur)


def _in_proj(x, g_mix, w_in, g_q, g_k, g_ln, b_ln, rope, seq):
    t = x.shape[0]
    tm = TM_PROJ
    tiles_per_seq = seq // tm
    row = lambda i: (i, 0)
    rope_spec = pl.BlockSpec((tm, HEAD_DIM), lambda i: (i % tiles_per_seq, 0))
    return pl.pallas_call(
        _in_proj_kernel,
        grid=(t // tm,),
        in_specs=[
            pl.BlockSpec((tm, D_MODEL), row),
            _const_spec((1, D_MODEL)),
            _const_spec((D_MODEL, IN_WIDTH)),
            _const_spec((1, HEAD_DIM)),
            _const_spec((1, HEAD_DIM)),
            _const_spec((1, GMLP_WIDTH)),
            _const_spec((1, GMLP_WIDTH)),
            rope_spec, rope_spec, rope_spec,
        ],
        out_specs=[
            pl.BlockSpec((tm, ATTN_WIDTH), row),
            pl.BlockSpec((tm, 2 * KV_WIDTH), row),
            pl.BlockSpec((tm, GMLP_WIDTH), row),
            pl.BlockSpec((tm, GMLP_WIDTH), row),
        ],
        out_shape=[
            jax.ShapeDtypeStruct((t, ATTN_WIDTH), BF16),
            jax.ShapeDtypeStruct((t, 2 * KV_WIDTH), BF16),
            jax.ShapeDtypeStruct((t, GMLP_WIDTH), BF16),
            jax.ShapeDtypeStruct((t, GMLP_WIDTH), BF16),
        ],
        compiler_params=pltpu.CompilerParams(
            dimension_semantics=("arbitrary",),
            vmem_limit_bytes=V7X_VMEM_LIMIT_BYTES),
        name="in_proj",
    )(x, g_mix, w_in, g_q, g_k, g_ln, b_ln, *rope)


def _mixer_kernel(tiles_per_seq, n_tiles, x_ref, q_ref, kvp_ref, kvc_ref, kvn_ref, u_ref,
                  vg_ref, sink_ref, wsp_ref, bsp_ref, gao_ref, ggo_ref, wout_ref, o_ref,
                  kv_buf, mix_buf, mixn_buf):
    tm = x_ref.shape[0]
    nblk = tm // BLOCK
    step = pl.program_id(0)
    tile_in_seq = jnp.minimum(step, n_tiles - 1) % tiles_per_seq
    is_first = tile_in_seq == 0
    is_last = tile_in_seq == tiles_per_seq - 1
    slot_w = step % 2
    slot_r = 1 - slot_w

    @pl.when(step == 0)
    def _():
        mixn_buf[1] = jnp.zeros(mixn_buf.shape[1:], BF16)

    kv_buf[0:BLOCK, :] = kvp_ref[...]
    kv_buf[BLOCK:BLOCK + tm, :] = kvc_ref[...]
    kv_buf[BLOCK + tm:, :] = kvn_ref[...]

    qi = lax.broadcasted_iota(jnp.int32, (BLOCK, BLOCK), 0)
    kj = lax.broadcasted_iota(jnp.int32, (BLOCK, BLOCK), 1)
    zero = jnp.zeros((BLOCK, BLOCK), F32)
    prev_bias = jnp.where(kj >= qi, 0.0, NEG_INF).astype(F32)
    next_bias = jnp.where(kj <= qi, 0.0, NEG_INF).astype(F32)
    scale = 1.0 / math.sqrt(HEAD_DIM)

    def band_bias(b):
        pb, nb_ = prev_bias, next_bias
        if b == 0:
            pb = jnp.where(is_first, NEG_INF, pb)
        if b == nblk - 1:
            nb_ = jnp.where(is_last, NEG_INF, nb_)
        return jnp.concatenate([pb, zero, nb_], axis=1)[None]

    def scores(b, kvh):
        r0 = b * BLOCK
        qs = jnp.concatenate(
            [q_ref[r0:r0 + BLOCK, (kvh * GQA_GROUP + g) * HEAD_DIM:
                   (kvh * GQA_GROUP + g + 1) * HEAD_DIM] for g in range(GQA_GROUP)],
            axis=0)
        kwin = kv_buf[r0:r0 + 3 * BLOCK, kvh * HEAD_DIM:(kvh + 1) * HEAD_DIM]
        return lax.dot_general(qs, kwin, (((1,), (1,)), ((), ())),
                               preferred_element_type=F32)

    def softmax_pv(b, kvh, s):
        r0 = b * BLOCK
        vwin = kv_buf[r0:r0 + 3 * BLOCK,
                      KV_WIDTH + kvh * HEAD_DIM:KV_WIDTH + (kvh + 1) * HEAD_DIM]
        s = (s * scale).reshape(GQA_GROUP, BLOCK, 3 * BLOCK) + band_bias(b)
        sink = jnp.concatenate(
            [jnp.full((1, 1, 1), sink_ref[kvh * GQA_GROUP + g], F32)
             for g in range(GQA_GROUP)], axis=0)
        m = jnp.maximum(jnp.max(s, axis=-1, keepdims=True), sink)
        p = jnp.exp(s - m)
        denom = jnp.sum(p, axis=-1, keepdims=True) + jnp.exp(sink - m)
        o = jnp.dot(p.reshape(GQA_GROUP * BLOCK, 3 * BLOCK).astype(BF16), vwin,
                    preferred_element_type=F32)
        o = o.reshape(GQA_GROUP, BLOCK, HEAD_DIM) * (1.0 / denom)
        for g in range(GQA_GROUP):
            c0 = (kvh * GQA_GROUP + g) * HEAD_DIM
            mix_buf[r0:r0 + BLOCK, c0:c0 + HEAD_DIM] = o[g]

    def spatial_gate(hd):
        c0 = hd * HEAD_DIM
        vh = jnp.concatenate(
            [vg_ref[c * BLOCK:(c + 1) * BLOCK, c0:c0 + HEAD_DIM] for c in range(nblk)], axis=1)
        mixed = jnp.dot(wsp_ref[hd], vh, preferred_element_type=F32)
        bias_h = bsp_ref[hd]
        for c in range(nblk):
            uu = u_ref[c * BLOCK:(c + 1) * BLOCK, c0:c0 + HEAD_DIM].astype(F32)
            mix_buf[c * BLOCK:(c + 1) * BLOCK, ATTN_WIDTH + c0:ATTN_WIDTH + c0 + HEAD_DIM] = (
                uu * (mixed[:, c * BLOCK:(c + 1) * BLOCK] + bias_h))

    chains = [(b, kvh) for b in range(nblk) for kvh in range(N_KV_HEADS)]
    slab = D_MODEL // len(chains)
    mix_prev = mixn_buf.at[slot_r]
    s_next = scores(*chains[0])
    for k, (b, kvh) in enumerate(chains):
        s_cur = s_next
        if k + 1 < len(chains):
            s_next = scores(*chains[k + 1])
        c0 = k * slab
        o_ref[:, c0:c0 + slab] = x_ref[:, c0:c0 + slab] + jnp.dot(
            mix_prev[...], wout_ref[:, c0:c0 + slab], preferred_element_type=F32)
        softmax_pv(b, kvh, s_cur)
        for hd in range(k * N_GMLP_HEADS // len(chains), (k + 1) * N_GMLP_HEADS // len(chains)):
            spatial_gate(hd)

    attn = mix_buf[:, :ATTN_WIDTH]
    gm = mix_buf[:, ATTN_WIDTH:]
    mixn_buf[slot_w, :, :ATTN_WIDTH] = (attn * _rms_scale(attn) * gao_ref[...]).astype(BF16)
    mixn_buf[slot_w, :, ATTN_WIDTH:] = (gm * _rms_scale(gm) * ggo_ref[...]).astype(BF16)


def _mixer(x, q, kv, u, vg, sink, w_sp, b_sp, g_ao, g_go, w_out, seq):
    t = x.shape[0]
    tm = TM_MIX
    nblk = tm // BLOCK
    n_tiles = t // tm
    tiles_per_seq = seq // tm
    last_block = t // BLOCK - 1
    cur = lambda i: jnp.minimum(i, n_tiles - 1)
    row = lambda i: (cur(i), 0)
    lag = lambda i: (jnp.maximum(i - 1, 0), 0)
    return pl.pallas_call(
        functools.partial(_mixer_kernel, tiles_per_seq, n_tiles),
        grid=(n_tiles + 1,),
        in_specs=[
            pl.BlockSpec((tm, D_MODEL), lag),
            pl.BlockSpec((tm, ATTN_WIDTH), row),
            pl.BlockSpec((BLOCK, 2 * KV_WIDTH),
                         lambda i: (jnp.maximum(cur(i) * nblk - 1, 0), 0)),
            pl.BlockSpec((tm, 2 * KV_WIDTH), row),
            pl.BlockSpec((BLOCK, 2 * KV_WIDTH),
                         lambda i: (jnp.minimum((cur(i) + 1) * nblk, last_block), 0)),
            pl.BlockSpec((tm, GMLP_WIDTH), row),
            pl.BlockSpec((tm, GMLP_WIDTH), row),
            pl.BlockSpec(memory_space=pltpu.SMEM),
            _const_spec((N_GMLP_HEADS, BLOCK, BLOCK)),
            _const_spec((N_GMLP_HEADS, BLOCK, BLOCK)),
            _const_spec((1, ATTN_WIDTH)),
            _const_spec((1, GMLP_WIDTH)),
            _const_spec((D_MODEL, D_MODEL)),
        ],
        out_specs=pl.BlockSpec((tm, D_MODEL), lag),
        out_shape=jax.ShapeDtypeStruct((t, D_MODEL), F32),
        scratch_shapes=[
            pltpu.VMEM((tm + 2 * BLOCK, 2 * KV_WIDTH), BF16),
            pltpu.VMEM((tm, D_MODEL), F32),
            pltpu.VMEM((2, tm, D_MODEL), BF16),
        ],
        compiler_params=pltpu.CompilerParams(
            dimension_semantics=("arbitrary",),
            vmem_limit_bytes=V7X_VMEM_LIMIT_BYTES),
        name="mixer",
    )(x, q, kv, kv, kv, u, vg, sink, w_sp, b_sp, g_ao, g_go, w_out)


def _ffn_kernel(x_ref, g_ref, wup_ref, wdn_ref, o_ref, h_ref):
    @pl.when(pl.program_id(1) == 0)
    def _():
        x = x_ref[...]
        h_ref[...] = (x * _rms_scale(x) * g_ref[...]).astype(BF16)
        o_ref[...] = x

    up = jnp.dot(h_ref[...], wup_ref[...].astype(BF16), preferred_element_type=F32)
    act = jnp.square(jnp.maximum(up, 0.0)).astype(BF16)
    o_ref[...] += jnp.dot(act, wdn_ref[...].astype(BF16), preferred_element_type=F32)


def _ffn(x, g_ffn, w_up, w_down):
    t = x.shape[0]
    tm, tf = TM_FFN, TF_FFN
    return pl.pallas_call(
        _ffn_kernel,
        grid=(t // tm, D_FF // tf),
        in_specs=[
            pl.BlockSpec((tm, D_MODEL), lambda i, f: (i, 0)),
            _const_spec((1, D_MODEL)),
            pl.BlockSpec((D_MODEL, tf), lambda i, f: (0, f)),
            pl.BlockSpec((tf, D_MODEL), lambda i, f: (f, 0)),
        ],
        out_specs=pl.BlockSpec((tm, D_MODEL), lambda i, f: (i, 0)),
        out_shape=jax.ShapeDtypeStruct((t, D_MODEL), F32),
        scratch_shapes=[pltpu.VMEM((tm, D_MODEL), BF16)],
        compiler_params=pltpu.CompilerParams(
            dimension_semantics=("arbitrary", "arbitrary"),
            vmem_limit_bytes=V7X_VMEM_LIMIT_BYTES),
        name="ffn",
    )(x, g_ffn, w_up, w_down)


def _rope_tables(seq):
    pos = jnp.arange(seq, dtype=F32)
    inv_freq = ROPE_THETA ** (-jnp.arange(0, ROT_DIM, 2, dtype=F32) / ROT_DIM)
    ang = pos[:, None] * inv_freq[None, :]
    cos, sin = jnp.cos(ang), jnp.sin(ang)
    half = ROT_DIM // 2
    pad = HEAD_DIM - ROT_DIM
    cos_t = jnp.concatenate([cos, cos, jnp.ones((seq, pad), F32)], axis=1)
    sina_t = jnp.concatenate([-sin, jnp.zeros((seq, HEAD_DIM - half), F32)], axis=1)
    sinb_t = jnp.concatenate([jnp.zeros((seq, half), F32), sin, jnp.zeros((seq, pad), F32)], axis=1)
    return cos_t, sina_t, sinb_t


def _encoder_layer(x, p):
    bsz, seq, _ = x.shape
    xt = x.reshape(bsz * seq, D_MODEL)
    q, kv, u, vg = _in_proj(xt, p["g_mix"], p["w_in"], p["g_q"], p["g_k"], p["g_v_ln"],
                            p["b_v_ln"], _rope_tables(seq), seq)
    x1 = _mixer(xt, q, kv, u, vg, p["sink"], p["w_spatial"], p["b_spatial"],
                p["g_attn_out"], p["g_gmlp_out"], p["w_out"], seq)
    y = _ffn(x1, p["g_ffn"], p["w_up"], p["w_down"])
    return y.reshape(bsz, seq, D_MODEL)


def kernel(x_prompt, x_sample, g_mix, w_in, g_q, g_k, sink, g_v_ln, b_v_ln, w_spatial, b_spatial, g_attn_out, g_gmlp_out, w_out, g_ffn, w_up, w_down):
    y_prompt, y_sample = x_prompt, x_sample
    for l in range(g_mix.shape[0]):
        p = {
            "g_mix": g_mix[l][None], "w_in": w_in[l].astype(BF16),
            "g_q": g_q[l][None], "g_k": g_k[l][None], "sink": sink[l],
            "g_v_ln": g_v_ln[l][None], "b_v_ln": b_v_ln[l][None],
            "w_spatial": w_spatial[l].astype(BF16),
            "b_spatial": jnp.broadcast_to(b_spatial[l][:, :, None],
                                          (N_GMLP_HEADS, BLOCK, BLOCK)),
            "g_attn_out": g_attn_out[l][None], "g_gmlp_out": g_gmlp_out[l][None],
            "w_out": w_out[l].astype(BF16), "g_ffn": g_ffn[l][None],
            "w_up": w_up[l], "w_down": w_down[l],
        }
        y_prompt = _encoder_layer(y_prompt, p)
        y_sample = _encoder_layer(y_sample, p)
    return (y_prompt, y_sample)
```

```python
import functools
import math

import jax
import jax.numpy as jnp
from jax import lax
from jax.experimental import pallas as pl
from jax.experimental.pallas import tpu as pltpu

D_MODEL = 2048
HEAD_DIM = 128
N_ATTN_HEADS = 8
N_KV_HEADS = 2
GQA_GROUP = N_ATTN_HEADS // N_KV_HEADS
ATTN_WIDTH = N_ATTN_HEADS * HEAD_DIM
KV_WIDTH = N_KV_HEADS * HEAD_DIM
N_GMLP_HEADS = 8
GMLP_WIDTH = N_GMLP_HEADS * HEAD_DIM
IN_WIDTH = ATTN_WIDTH + 2 * KV_WIDTH + 2 * GMLP_WIDTH
BLOCK = 128
ROPE_THETA = 500000.0
ROT_DIM = HEAD_DIM // 4
D_FF = 4 * D_MODEL
EPS = 1e-6
NEG_INF = -1e30

Q_OFF = 0
KV_OFF = ATTN_WIDTH
U_OFF = KV_OFF + 2 * KV_WIDTH
VG_OFF = U_OFF + GMLP_WIDTH

TM_PROJ = 512
ROW_PARTS_PROJ = 4
TM_MIX = 512
TM_FFN = 1024
TF_FFN = 512

V7X_VMEM_LIMIT_BYTES = 56 * 1024 * 1024

F32 = jnp.float32
BF16 = jnp.bfloat16


def _rms_scale(x):
    return lax.rsqrt(jnp.mean(x * x, axis=-1, keepdims=True) + EPS)


def _const_spec(shape):
    nd = len(shape)
    return pl.BlockSpec(shape, lambda *_: (0,) * nd, pipeline_mode=pl.Buffered(1))


def _in_proj_kernel(x_ref, gmix_ref, w_ref, gq_ref, gk_ref, gln_ref, bln_ref,
                    cos_ref, sina_ref, sinb_ref, wcast_ref,
                    q_ref, kv_ref, u_ref, vg_ref, wcast_out_ref):
    tm = x_ref.shape[0]
    rows_per_part = tm // ROW_PARTS_PROJ

    wcast_out_ref[...] = wcast_ref[...].astype(BF16)

    def normed(r0):
        x = x_ref[r0:r0 + rows_per_part, :]
        return (x * _rms_scale(x) * gmix_ref[...]).astype(BF16)

    def norm_rope(r0, xh, g):
        rows = slice(r0, r0 + rows_per_part)
        y = xh * _rms_scale(xh) * g
        return (y * cos_ref[rows, :]
                + pltpu.roll(y, HEAD_DIM - ROT_DIM // 2, 1) * sina_ref[rows, :]
                + pltpu.roll(y, ROT_DIM // 2, 1) * sinb_ref[rows, :])

    def vg_epilogue(r0, p):
        p = jax.nn.gelu(p)
        mu = jnp.mean(p, axis=-1, keepdims=True)
        pc = p - mu
        y = pc * lax.rsqrt(jnp.mean(pc * pc, axis=-1, keepdims=True) + EPS)
        vg_ref[r0:r0 + rows_per_part, :] = (y * gln_ref[...] + bln_ref[...]).astype(BF16)

    def u_epilogue(c0, r0, p):
        u_ref[r0:r0 + rows_per_part, c0:c0 + p.shape[1]] = jax.nn.gelu(p).astype(BF16)

    def q_epilogue(c0, r0, p):
        g = gq_ref[...]
        for lo in range(0, p.shape[1], HEAD_DIM):
            q_ref[r0:r0 + rows_per_part, c0 + lo:c0 + lo + HEAD_DIM] = norm_rope(
                r0, p[:, lo:lo + HEAD_DIM], g).astype(BF16)

    def kv_epilogue(r0, p):
        g = gk_ref[...]
        for lo in range(0, KV_WIDTH, HEAD_DIM):
            kv_ref[r0:r0 + rows_per_part, lo:lo + HEAD_DIM] = norm_rope(
                r0, p[:, lo:lo + HEAD_DIM], g).astype(BF16)
        kv_ref[r0:r0 + rows_per_part, KV_WIDTH:] = p[:, KV_WIDTH:].astype(BF16)

    half_q, half_u = ATTN_WIDTH // 2, GMLP_WIDTH // 2
    groups = [
        (Q_OFF, half_q, functools.partial(q_epilogue, 0)),
        (Q_OFF + half_q, half_q, functools.partial(q_epilogue, half_q)),
        (KV_OFF, 2 * KV_WIDTH, kv_epilogue),
        (U_OFF, half_u, functools.partial(u_epilogue, 0)),
        (U_OFF + half_u, half_u, functools.partial(u_epilogue, half_u)),
        (VG_OFF, GMLP_WIDTH, vg_epilogue),
    ]
    stages = [(part, g) for part in range(ROW_PARTS_PROJ) for g in range(len(groups))]
    h = {0: normed(0)}

    def proj(stage):
        part, g = stage
        col, width, _ = groups[g]
        return jnp.dot(h[part], w_ref[:, col:col + width], preferred_element_type=F32)

    p_next = proj(stages[0])
    for k, (part, g) in enumerate(stages):
        p_cur = p_next
        if g == 0 and part + 1 < ROW_PARTS_PROJ:
            h[part + 1] = normed((part + 1) * rows_per_part)
        if k + 1 < len(stages):
            p_next = proj(stages[k + 1])
        groups[g][2](part * rows_per_part, p_cur)


def _in_proj(x, g_mix, w_in, g_q, g_k, g_ln, b_ln, rope, seq, w_cast):
    t = x.shape[0]
    tm = TM_PROJ
    n_tiles = t // tm
    tiles_per_seq = seq // tm
    row = lambda i: (i, 0)
    cast_rows, cast_cols = w_cast.shape
    cast_spec = pl.BlockSpec((cast_rows // n_tiles, cast_cols), row)
    rope_spec = pl.BlockSpec((tm, HEAD_DIM), lambda i: (i % tiles_per_seq, 0))
    return pl.pallas_call(
        _in_proj_kernel,
        grid=(t // tm,),
        in_specs=[
            pl.BlockSpec((tm, D_MODEL), row),
            _const_spec((1, D_MODEL)),
            _const_spec((D_MODEL, IN_WIDTH)),
            _const_spec((1, HEAD_DIM)),
            _const_spec((1, HEAD_DIM)),
            _const_spec((1, GMLP_WIDTH)),
            _const_spec((1, GMLP_WIDTH)),
            rope_spec, rope_spec, rope_spec,
            cast_spec,
        ],
        out_specs=[
            pl.BlockSpec((tm, ATTN_WIDTH), row),
            pl.BlockSpec((tm, 2 * KV_WIDTH), row),
            pl.BlockSpec((tm, GMLP_WIDTH), row),
            pl.BlockSpec((tm, GMLP_WIDTH), row),
            cast_spec,
        ],
        out_shape=[
            jax.ShapeDtypeStruct((t, ATTN_WIDTH), BF16),
            jax.ShapeDtypeStruct((t, 2 * KV_WIDTH), BF16),
            jax.ShapeDtypeStruct((t, GMLP_WIDTH), BF16),
            jax.ShapeDtypeStruct((t, GMLP_WIDTH), BF16),
            jax.ShapeDtypeStruct(w_cast.shape, BF16),
        ],
        compiler_params=pltpu.CompilerParams(
            dimension_semantics=("arbitrary",),
            vmem_limit_bytes=V7X_VMEM_LIMIT_BYTES),
        name="in_proj",
    )(x, g_mix, w_in, g_q, g_k, g_ln, b_ln, *rope, w_cast)


def _mixer_kernel(tiles_per_seq, n_tiles, x_ref, q_ref, kvp_ref, kvc_ref, kvn_ref, u_ref,
                  vg_ref, sink_ref, wsp_ref, bsp_ref, gao_ref, ggo_ref, wout_ref, gffn_ref,
                  o_ref, xg_ref, kv_buf, mix_buf, mixn_buf):
    tm = x_ref.shape[0]
    nblk = tm // BLOCK
    step = pl.program_id(0)
    tile_in_seq = jnp.minimum(step, n_tiles - 1) % tiles_per_seq
    is_first = tile_in_seq == 0
    is_last = tile_in_seq == tiles_per_seq - 1
    slot_w = step % 2
    slot_r = 1 - slot_w

    @pl.when(step == 0)
    def _():
        mixn_buf[1] = jnp.zeros(mixn_buf.shape[1:], BF16)

    kv_buf[0:BLOCK, :] = kvp_ref[...]
    kv_buf[BLOCK:BLOCK + tm, :] = kvc_ref[...]
    kv_buf[BLOCK + tm:, :] = kvn_ref[...]

    qi = lax.broadcasted_iota(jnp.int32, (BLOCK, BLOCK), 0)
    kj = lax.broadcasted_iota(jnp.int32, (BLOCK, BLOCK), 1)
    zero = jnp.zeros((BLOCK, BLOCK), F32)
    prev_bias = jnp.where(kj >= qi, 0.0, NEG_INF).astype(F32)
    next_bias = jnp.where(kj <= qi, 0.0, NEG_INF).astype(F32)
    scale = 1.0 / math.sqrt(HEAD_DIM)

    def band_bias(b):
        pb, nb_ = prev_bias, next_bias
        if b == 0:
            pb = jnp.where(is_first, NEG_INF, pb)
        if b == nblk - 1:
            nb_ = jnp.where(is_last, NEG_INF, nb_)
        return jnp.concatenate([pb, zero, nb_], axis=1)[None]

    def scores(b, kvh):
        r0 = b * BLOCK
        qs = jnp.concatenate(
            [q_ref[r0:r0 + BLOCK, (kvh * GQA_GROUP + g) * HEAD_DIM:
                   (kvh * GQA_GROUP + g + 1) * HEAD_DIM] for g in range(GQA_GROUP)],
            axis=0)
        kwin = kv_buf[r0:r0 + 3 * BLOCK, kvh * HEAD_DIM:(kvh + 1) * HEAD_DIM]
        return lax.dot_general(qs, kwin, (((1,), (1,)), ((), ())),
                               preferred_element_type=F32)

    def softmax_pv(b, kvh, s):
        r0 = b * BLOCK
        vwin = kv_buf[r0:r0 + 3 * BLOCK,
                      KV_WIDTH + kvh * HEAD_DIM:KV_WIDTH + (kvh + 1) * HEAD_DIM]
        s = (s * scale).reshape(GQA_GROUP, BLOCK, 3 * BLOCK) + band_bias(b)
        sink = jnp.concatenate(
            [jnp.full((1, 1, 1), sink_ref[kvh * GQA_GROUP + g], F32)
             for g in range(GQA_GROUP)], axis=0)
        m = jnp.maximum(jnp.max(s, axis=-1, keepdims=True), sink)
        p = jnp.exp(s - m)
        denom = jnp.sum(p, axis=-1, keepdims=True) + jnp.exp(sink - m)
        o = jnp.dot(p.reshape(GQA_GROUP * BLOCK, 3 * BLOCK).astype(BF16), vwin,
                    preferred_element_type=F32)
        o = o.reshape(GQA_GROUP, BLOCK, HEAD_DIM) * (1.0 / denom)
        sq = jnp.zeros((BLOCK, HEAD_DIM), F32)
        for g in range(GQA_GROUP):
            c0 = (kvh * GQA_GROUP + g) * HEAD_DIM
            mix_buf[r0:r0 + BLOCK, c0:c0 + HEAD_DIM] = o[g]
            sq = sq + jnp.square(o[g])
        return sq

    def spatial_gate(hd):
        c0 = hd * HEAD_DIM
        vh = jnp.concatenate(
            [vg_ref[c * BLOCK:(c + 1) * BLOCK, c0:c0 + HEAD_DIM] for c in range(nblk)], axis=1)
        mixed = jnp.dot(wsp_ref[hd], vh, preferred_element_type=F32)
        bias_h = bsp_ref[hd]
        sq = []
        for c in range(nblk):
            uu = u_ref[c * BLOCK:(c + 1) * BLOCK, c0:c0 + HEAD_DIM].astype(F32)
            gated = uu * (mixed[:, c * BLOCK:(c + 1) * BLOCK] + bias_h)
            mix_buf[c * BLOCK:(c + 1) * BLOCK, ATTN_WIDTH + c0:ATTN_WIDTH + c0 + HEAD_DIM] = gated
            sq.append(jnp.square(gated))
        return jnp.concatenate(sq, axis=0)

    def group_norm_to(rows, col0, width, sumsq, g_ref):
        r = lax.rsqrt(jnp.sum(sumsq, axis=-1, keepdims=True) * (1.0 / width) + EPS)
        mixn_buf[slot_w, rows, col0:col0 + width] = (
            mix_buf[rows, col0:col0 + width] * r * g_ref[...]).astype(BF16)

    chains = [(b, kvh) for b in range(nblk) for kvh in range(N_KV_HEADS)]
    slab = D_MODEL // len(chains)
    gates_per_chain = 2
    mix_prev = mixn_buf.at[slot_r]
    sumsq_gm = jnp.zeros((tm, HEAD_DIM), F32)
    sumsq_attn = jnp.zeros((BLOCK, HEAD_DIM), F32)
    gates_done = 0
    s_next = scores(*chains[0])
    for k, (b, kvh) in enumerate(chains):
        s_cur = s_next
        if k + 1 < len(chains):
            s_next = scores(*chains[k + 1])
        c0 = k * slab
        x1 = x_ref[:, c0:c0 + slab] + jnp.dot(
            mix_prev[...], wout_ref[:, c0:c0 + slab], preferred_element_type=F32)
        o_ref[:, c0:c0 + slab] = x1
        xg_ref[:, c0:c0 + slab] = (x1 * gffn_ref[:, c0:c0 + slab]).astype(BF16)

        sq = softmax_pv(b, kvh, s_cur)
        sumsq_attn = sq if kvh == 0 else sumsq_attn + sq
        if kvh == N_KV_HEADS - 1:
            group_norm_to(slice(b * BLOCK, (b + 1) * BLOCK), 0, ATTN_WIDTH, sumsq_attn, gao_ref)

        for hd in range(gates_done, min(gates_done + gates_per_chain, N_GMLP_HEADS)):
            sumsq_gm = sumsq_gm + spatial_gate(hd)
        if gates_done < N_GMLP_HEADS <= gates_done + gates_per_chain:
            group_norm_to(slice(0, tm), ATTN_WIDTH, GMLP_WIDTH, sumsq_gm, ggo_ref)
        gates_done += gates_per_chain


def _mixer(x, q, kv, u, vg, sink, w_sp, b_sp, g_ao, g_go, w_out, g_ffn, seq):
    t = x.shape[0]
    tm = TM_MIX
    nblk = tm // BLOCK
    n_tiles = t // tm
    tiles_per_seq = seq // tm
    last_block = t // BLOCK - 1
    cur = lambda i: jnp.minimum(i, n_tiles - 1)
    row = lambda i: (cur(i), 0)
    lag = lambda i: (jnp.maximum(i - 1, 0), 0)
    return pl.pallas_call(
        functools.partial(_mixer_kernel, tiles_per_seq, n_tiles),
        grid=(n_tiles + 1,),
        in_specs=[
            pl.BlockSpec((tm, D_MODEL), lag),
            pl.BlockSpec((tm, ATTN_WIDTH), row),
            pl.BlockSpec((BLOCK, 2 * KV_WIDTH),
                         lambda i: (jnp.maximum(cur(i) * nblk - 1, 0), 0)),
            pl.BlockSpec((tm, 2 * KV_WIDTH), row),
            pl.BlockSpec((BLOCK, 2 * KV_WIDTH),
                         lambda i: (jnp.minimum((cur(i) + 1) * nblk, last_block), 0)),
            pl.BlockSpec((tm, GMLP_WIDTH), row),
            pl.BlockSpec((tm, GMLP_WIDTH), row),
            pl.BlockSpec(memory_space=pltpu.SMEM),
            _const_spec((N_GMLP_HEADS, BLOCK, BLOCK)),
            _const_spec((N_GMLP_HEADS, BLOCK, BLOCK)),
            _const_spec((1, ATTN_WIDTH)),
            _const_spec((1, GMLP_WIDTH)),
            _const_spec((D_MODEL, D_MODEL)),
            _const_spec((1, D_MODEL)),
        ],
        out_specs=[pl.BlockSpec((tm, D_MODEL), lag), pl.BlockSpec((tm, D_MODEL), lag)],
        out_shape=[jax.ShapeDtypeStruct((t, D_MODEL), F32),
                   jax.ShapeDtypeStruct((t, D_MODEL), BF16)],
        scratch_shapes=[
            pltpu.VMEM((tm + 2 * BLOCK, 2 * KV_WIDTH), BF16),
            pltpu.VMEM((tm, D_MODEL), F32),
            pltpu.VMEM((2, tm, D_MODEL), BF16),
        ],
        compiler_params=pltpu.CompilerParams(
            dimension_semantics=("arbitrary",),
            vmem_limit_bytes=V7X_VMEM_LIMIT_BYTES),
        name="mixer",
    )(x, q, kv, kv, kv, u, vg, sink, w_sp, b_sp, g_ao, g_go, w_out, g_ffn)


def _ffn_kernel(n_f, xg_ref, x_ref, wup_ref, wdn_ref, o_ref, act_buf):
    s = pl.program_id(0)
    slot_up = s % 2
    slot_dn = 1 - slot_up
    f_dn = jnp.maximum(s - 1, 0) % n_f

    def up_stage():
        up = jnp.dot(xg_ref[...], wup_ref[...], preferred_element_type=F32)
        act_buf[slot_up] = jnp.square(jnp.maximum(up, 0.0)).astype(BF16)

    def down():
        return jnp.dot(act_buf[slot_dn], wdn_ref[...], preferred_element_type=F32)

    @pl.when(s == 0)
    def _():
        up_stage()

    @pl.when((s > 0) & (f_dn == 0))
    def _():
        up_stage()
        o_ref[...] = down()

    @pl.when((s > 0) & (f_dn > 0) & (f_dn < n_f - 1))
    def _():
        up_stage()
        o_ref[...] += down()

    @pl.when((s > 0) & (f_dn == n_f - 1))
    def _():
        up_stage()
        x = x_ref[...]
        r2 = 1.0 / (jnp.mean(x * x, axis=-1, keepdims=True) + EPS)
        o_ref[...] = x + r2 * (o_ref[...] + down())


def _ffn(xg, x, w_up, w_down):
    t = x.shape[0]
    tm, tf = TM_FFN, TF_FFN
    n_f = D_FF // tf
    n_items = (t // tm) * n_f
    up_item = lambda s: jnp.minimum(s, n_items - 1)
    dn_item = lambda s: jnp.maximum(s - 1, 0)
    return pl.pallas_call(
        functools.partial(_ffn_kernel, n_f),
        grid=(n_items + 1,),
        in_specs=[
            pl.BlockSpec((tm, D_MODEL), lambda s: (up_item(s) // n_f, 0)),
            pl.BlockSpec((tm, D_MODEL), lambda s: (dn_item(s) // n_f, 0)),
            pl.BlockSpec((D_MODEL, tf), lambda s: (0, up_item(s) % n_f)),
            pl.BlockSpec((tf, D_MODEL), lambda s: (dn_item(s) % n_f, 0)),
        ],
        out_specs=pl.BlockSpec((tm, D_MODEL), lambda s: (dn_item(s) // n_f, 0)),
        out_shape=jax.ShapeDtypeStruct((t, D_MODEL), F32),
        scratch_shapes=[pltpu.VMEM((2, tm, tf), BF16)],
        compiler_params=pltpu.CompilerParams(
            dimension_semantics=("arbitrary",),
            vmem_limit_bytes=V7X_VMEM_LIMIT_BYTES),
        name="ffn",
    )(xg, x, w_up, w_down)


def _rope_tables(seq):
    pos = jnp.arange(seq, dtype=F32)
    inv_freq = ROPE_THETA ** (-jnp.arange(0, ROT_DIM, 2, dtype=F32) / ROT_DIM)
    ang = pos[:, None] * inv_freq[None, :]
    cos, sin = jnp.cos(ang), jnp.sin(ang)
    half = ROT_DIM // 2
    pad = HEAD_DIM - ROT_DIM
    cos_t = jnp.concatenate([cos, cos, jnp.ones((seq, pad), F32)], axis=1)
    sina_t = jnp.concatenate([-sin, jnp.zeros((seq, HEAD_DIM - half), F32)], axis=1)
    sinb_t = jnp.concatenate([jnp.zeros((seq, half), F32), sin, jnp.zeros((seq, pad), F32)], axis=1)
    return cos_t, sina_t, sinb_t


def kernel(x_prompt, x_sample, g_mix, w_in, g_q, g_k, sink, g_v_ln, b_v_ln, w_spatial, b_spatial, g_attn_out, g_gmlp_out, w_out, g_ffn, w_up, w_down):
    streams = [x_prompt, x_sample]
    for l in range(g_mix.shape[0]):
        w_in_l = w_in[l].astype(BF16)
        w_sp_l = w_spatial[l].astype(BF16)
        b_sp_l = jnp.broadcast_to(b_spatial[l][:, :, None], (N_GMLP_HEADS, BLOCK, BLOCK))
        w_out_l = w_out[l].astype(BF16)
        mlp_weights = [w_up[l], w_down[l]]
        proj_out, mlp_weights_bf16 = [], []
        for x, w_cast in zip(streams, mlp_weights):
            seq = x.shape[1]
            *qkuv, w_bf16 = _in_proj(
                x.reshape(-1, D_MODEL), g_mix[l][None], w_in_l, g_q[l][None], g_k[l][None],
                g_v_ln[l][None], b_v_ln[l][None], _rope_tables(seq), seq, w_cast)
            proj_out.append(qkuv)
            mlp_weights_bf16.append(w_bf16)
        mixed = [
            _mixer(x.reshape(-1, D_MODEL), *qkuv, sink[l], w_sp_l, b_sp_l, g_attn_out[l][None],
                   g_gmlp_out[l][None], w_out_l, g_ffn[l][None], x.shape[1])
            for x, qkuv in zip(streams, proj_out)]
        streams = [
            _ffn(xg, x1, *mlp_weights_bf16).reshape(x.shape)
            for x, (x1, xg) in zip(streams, mixed)]
    return tuple(streams)
```

```python
import functools
import math

import jax
import jax.numpy as jnp
from jax import lax
from jax.experimental import pallas as pl
from jax.experimental.pallas import tpu as pltpu

D_MODEL = 2048
HEAD_DIM = 128
N_ATTN_HEADS = 8
N_KV_HEADS = 2
GQA_GROUP = N_ATTN_HEADS // N_KV_HEADS
ATTN_WIDTH = N_ATTN_HEADS * HEAD_DIM
KV_WIDTH = N_KV_HEADS * HEAD_DIM
N_GMLP_HEADS = 8
GMLP_WIDTH = N_GMLP_HEADS * HEAD_DIM
IN_WIDTH = ATTN_WIDTH + 2 * KV_WIDTH + 2 * GMLP_WIDTH
BLOCK = 128
ROPE_THETA = 500000.0
ROT_DIM = HEAD_DIM // 4
D_FF = 4 * D_MODEL
EPS = 1e-6
NEG_INF = -1e30

Q_OFF = 0
KV_OFF = ATTN_WIDTH
U_OFF = KV_OFF + 2 * KV_WIDTH
VG_OFF = U_OFF + GMLP_WIDTH

TM_PROJ = 512
ROW_PARTS_PROJ = 4
TM_MIX = 512
TM_FFN = 512
TF_FFN = 2048
TF_SUB = 512

V7X_VMEM_LIMIT_BYTES = 56 * 1024 * 1024
V7X_VMEM_LIMIT_FFN_BYTES = 60 * 1024 * 1024

F32 = jnp.float32
BF16 = jnp.bfloat16


def _rms_scale(x):
    return lax.rsqrt(jnp.mean(x * x, axis=-1, keepdims=True) + EPS)


def _const_spec(shape):
    nd = len(shape)
    return pl.BlockSpec(shape, lambda *_: (0,) * nd, pipeline_mode=pl.Buffered(1))


def _in_proj_kernel(x_ref, gmix_ref, w_ref, gq_ref, gk_ref, gln_ref, bln_ref,
                    cos_ref, sina_ref, sinb_ref, wcast_ref,
                    q_ref, kv_ref, u_ref, vg_ref, wcast_out_ref):
    tm = x_ref.shape[0]
    rows_per_part = tm // ROW_PARTS_PROJ

    wcast_out_ref[...] = wcast_ref[...].astype(BF16)

    def normed(r0):
        x = x_ref[r0:r0 + rows_per_part, :]
        return (x * _rms_scale(x) * gmix_ref[...]).astype(BF16)

    def norm_rope(r0, xh, g):
        rows = slice(r0, r0 + rows_per_part)
        y = xh * _rms_scale(xh) * g
        return (y * cos_ref[rows, :]
                + pltpu.roll(y, HEAD_DIM - ROT_DIM // 2, 1) * sina_ref[rows, :]
                + pltpu.roll(y, ROT_DIM // 2, 1) * sinb_ref[rows, :])

    def vg_epilogue(r0, p):
        p = jax.nn.gelu(p)
        mu = jnp.mean(p, axis=-1, keepdims=True)
        pc = p - mu
        y = pc * lax.rsqrt(jnp.mean(pc * pc, axis=-1, keepdims=True) + EPS)
        vg_ref[r0:r0 + rows_per_part, :] = (y * gln_ref[...] + bln_ref[...]).astype(BF16)

    def u_epilogue(c0, r0, p):
        u_ref[r0:r0 + rows_per_part, c0:c0 + p.shape[1]] = jax.nn.gelu(p).astype(BF16)

    def q_epilogue(c0, r0, p):
        g = gq_ref[...]
        for lo in range(0, p.shape[1], HEAD_DIM):
            q_ref[r0:r0 + rows_per_part, c0 + lo:c0 + lo + HEAD_DIM] = norm_rope(
                r0, p[:, lo:lo + HEAD_DIM], g).astype(BF16)

    def kv_epilogue(r0, p):
        g = gk_ref[...]
        for lo in range(0, KV_WIDTH, HEAD_DIM):
            kv_ref[r0:r0 + rows_per_part, lo:lo + HEAD_DIM] = norm_rope(
                r0, p[:, lo:lo + HEAD_DIM], g).astype(BF16)
        kv_ref[r0:r0 + rows_per_part, KV_WIDTH:] = p[:, KV_WIDTH:].astype(BF16)

    half_q, half_u = ATTN_WIDTH // 2, GMLP_WIDTH // 2
    groups = [
        (Q_OFF, half_q, functools.partial(q_epilogue, 0)),
        (Q_OFF + half_q, half_q, functools.partial(q_epilogue, half_q)),
        (KV_OFF, 2 * KV_WIDTH, kv_epilogue),
        (U_OFF, half_u, functools.partial(u_epilogue, 0)),
        (U_OFF + half_u, half_u, functools.partial(u_epilogue, half_u)),
        (VG_OFF, GMLP_WIDTH, vg_epilogue),
    ]
    stages = [(part, g) for part in range(ROW_PARTS_PROJ) for g in range(len(groups))]
    h = {0: normed(0)}

    def proj(stage):
        part, g = stage
        col, width, _ = groups[g]
        return jnp.dot(h[part], w_ref[:, col:col + width], preferred_element_type=F32)

    p_next = proj(stages[0])
    for k, (part, g) in enumerate(stages):
        p_cur = p_next
        if g == 0 and part + 1 < ROW_PARTS_PROJ:
            h[part + 1] = normed((part + 1) * rows_per_part)
        if k + 1 < len(stages):
            p_next = proj(stages[k + 1])
        groups[g][2](part * rows_per_part, p_cur)


def _in_proj(x, g_mix, w_in, g_q, g_k, g_ln, b_ln, rope, seq, w_cast):
    t = x.shape[0]
    tm = TM_PROJ
    n_tiles = t // tm
    tiles_per_seq = seq // tm
    row = lambda i: (i, 0)
    cast_rows, cast_cols = w_cast.shape
    cast_spec = pl.BlockSpec((cast_rows // n_tiles, cast_cols), row)
    rope_spec = pl.BlockSpec((tm, HEAD_DIM), lambda i: (i % tiles_per_seq, 0))
    return pl.pallas_call(
        _in_proj_kernel,
        grid=(t // tm,),
        in_specs=[
            pl.BlockSpec((tm, D_MODEL), row),
            _const_spec((1, D_MODEL)),
            _const_spec((D_MODEL, IN_WIDTH)),
            _const_spec((1, HEAD_DIM)),
            _const_spec((1, HEAD_DIM)),
            _const_spec((1, GMLP_WIDTH)),
            _const_spec((1, GMLP_WIDTH)),
            rope_spec, rope_spec, rope_spec,
            cast_spec,
        ],
        out_specs=[
            pl.BlockSpec((tm, ATTN_WIDTH), row),
            pl.BlockSpec((tm, 2 * KV_WIDTH), row),
            pl.BlockSpec((tm, GMLP_WIDTH), row),
            pl.BlockSpec((tm, GMLP_WIDTH), row),
            cast_spec,
        ],
        out_shape=[
            jax.ShapeDtypeStruct((t, ATTN_WIDTH), BF16),
            jax.ShapeDtypeStruct((t, 2 * KV_WIDTH), BF16),
            jax.ShapeDtypeStruct((t, GMLP_WIDTH), BF16),
            jax.ShapeDtypeStruct((t, GMLP_WIDTH), BF16),
            jax.ShapeDtypeStruct(w_cast.shape, BF16),
        ],
        compiler_params=pltpu.CompilerParams(
            dimension_semantics=("arbitrary",),
            vmem_limit_bytes=V7X_VMEM_LIMIT_BYTES),
        name="in_proj",
    )(x, g_mix, w_in, g_q, g_k, g_ln, b_ln, *rope, w_cast)


def _mixer_kernel(tiles_per_seq, n_tiles, x_ref, q_ref, kvp_ref, kvc_ref, kvn_ref, u_ref,
                  vg_ref, sink_ref, wsp_ref, bsp_ref, gao_ref, ggo_ref, wout_ref, gffn_ref,
                  o_ref, xg_ref, kv_buf, mix_buf, mixn_buf):
    tm = x_ref.shape[0]
    nblk = tm // BLOCK
    step = pl.program_id(0)
    tile_in_seq = jnp.minimum(step, n_tiles - 1) % tiles_per_seq
    is_first = tile_in_seq == 0
    is_last = tile_in_seq == tiles_per_seq - 1
    slot_w = step % 2
    slot_r = 1 - slot_w

    @pl.when(step == 0)
    def _():
        mixn_buf[1] = jnp.zeros(mixn_buf.shape[1:], BF16)

    kv_buf[0:BLOCK, :] = kvp_ref[...]
    kv_buf[BLOCK:BLOCK + tm, :] = kvc_ref[...]
    kv_buf[BLOCK + tm:, :] = kvn_ref[...]

    qi = lax.broadcasted_iota(jnp.int32, (BLOCK, BLOCK), 0)
    kj = lax.broadcasted_iota(jnp.int32, (BLOCK, BLOCK), 1)
    zero = jnp.zeros((BLOCK, BLOCK), F32)
    prev_bias = jnp.where(kj >= qi, 0.0, NEG_INF).astype(F32)
    next_bias = jnp.where(kj <= qi, 0.0, NEG_INF).astype(F32)
    scale = 1.0 / math.sqrt(HEAD_DIM)

    def band_bias(b):
        pb, nb_ = prev_bias, next_bias
        if b == 0:
            pb = jnp.where(is_first, NEG_INF, pb)
        if b == nblk - 1:
            nb_ = jnp.where(is_last, NEG_INF, nb_)
        return jnp.concatenate([pb, zero, nb_], axis=1)[None]

    def scores(b, kvh):
        r0 = b * BLOCK
        qs = jnp.concatenate(
            [q_ref[r0:r0 + BLOCK, (kvh * GQA_GROUP + g) * HEAD_DIM:
                   (kvh * GQA_GROUP + g + 1) * HEAD_DIM] for g in range(GQA_GROUP)],
            axis=0)
        kwin = kv_buf[r0:r0 + 3 * BLOCK, kvh * HEAD_DIM:(kvh + 1) * HEAD_DIM]
        return lax.dot_general(qs, kwin, (((1,), (1,)), ((), ())),
                               preferred_element_type=F32)

    def softmax_pv(b, kvh, s):
        r0 = b * BLOCK
        vwin = kv_buf[r0:r0 + 3 * BLOCK,
                      KV_WIDTH + kvh * HEAD_DIM:KV_WIDTH + (kvh + 1) * HEAD_DIM]
        s = (s * scale).reshape(GQA_GROUP, BLOCK, 3 * BLOCK) + band_bias(b)
        sink = jnp.concatenate(
            [jnp.full((1, 1, 1), sink_ref[kvh * GQA_GROUP + g], F32)
             for g in range(GQA_GROUP)], axis=0)
        m = jnp.maximum(jnp.max(s, axis=-1, keepdims=True), sink)
        p = jnp.exp(s - m)
        denom = jnp.sum(p, axis=-1, keepdims=True) + jnp.exp(sink - m)
        o = jnp.dot(p.reshape(GQA_GROUP * BLOCK, 3 * BLOCK).astype(BF16), vwin,
                    preferred_element_type=F32)
        o = o.reshape(GQA_GROUP, BLOCK, HEAD_DIM) * (1.0 / denom)
        sq = jnp.zeros((BLOCK, HEAD_DIM), F32)
        for g in range(GQA_GROUP):
            c0 = (kvh * GQA_GROUP + g) * HEAD_DIM
            mix_buf[r0:r0 + BLOCK, c0:c0 + HEAD_DIM] = o[g]
            sq = sq + jnp.square(o[g])
        return sq

    def spatial_gate(hd):
        c0 = hd * HEAD_DIM
        vh = jnp.concatenate(
            [vg_ref[c * BLOCK:(c + 1) * BLOCK, c0:c0 + HEAD_DIM] for c in range(nblk)], axis=1)
        mixed = jnp.dot(wsp_ref[hd], vh, preferred_element_type=F32)
        bias_h = bsp_ref[hd]
        sq = []
        for c in range(nblk):
            uu = u_ref[c * BLOCK:(c + 1) * BLOCK, c0:c0 + HEAD_DIM].astype(F32)
            gated = uu * (mixed[:, c * BLOCK:(c + 1) * BLOCK] + bias_h)
            mix_buf[c * BLOCK:(c + 1) * BLOCK, ATTN_WIDTH + c0:ATTN_WIDTH + c0 + HEAD_DIM] = gated
            sq.append(jnp.square(gated))
        return jnp.concatenate(sq, axis=0)

    def group_norm_to(rows, col0, width, sumsq, g_ref):
        r = lax.rsqrt(jnp.sum(sumsq, axis=-1, keepdims=True) * (1.0 / width) + EPS)
        mixn_buf[slot_w, rows, col0:col0 + width] = (
            mix_buf[rows, col0:col0 + width] * r * g_ref[...]).astype(BF16)

    chains = [(b, kvh) for b in range(nblk) for kvh in range(N_KV_HEADS)]
    slab = D_MODEL // len(chains)
    gates_per_chain = 2
    mix_prev = mixn_buf.at[slot_r]
    sumsq_gm = jnp.zeros((tm, HEAD_DIM), F32)
    sumsq_attn = jnp.zeros((BLOCK, HEAD_DIM), F32)
    gates_done = 0
    s_next = scores(*chains[0])
    for k, (b, kvh) in enumerate(chains):
        s_cur = s_next
        if k + 1 < len(chains):
            s_next = scores(*chains[k + 1])
        c0 = k * slab
        x1 = x_ref[:, c0:c0 + slab] + jnp.dot(
            mix_prev[...], wout_ref[:, c0:c0 + slab], preferred_element_type=F32)
        o_ref[:, c0:c0 + slab] = x1
        xg_ref[:, c0:c0 + slab] = (x1 * gffn_ref[:, c0:c0 + slab]).astype(BF16)

        sq = softmax_pv(b, kvh, s_cur)
        sumsq_attn = sq if kvh == 0 else sumsq_attn + sq
        if kvh == N_KV_HEADS - 1:
            group_norm_to(slice(b * BLOCK, (b + 1) * BLOCK), 0, ATTN_WIDTH, sumsq_attn, gao_ref)

        for hd in range(gates_done, min(gates_done + gates_per_chain, N_GMLP_HEADS)):
            sumsq_gm = sumsq_gm + spatial_gate(hd)
        if gates_done < N_GMLP_HEADS <= gates_done + gates_per_chain:
            group_norm_to(slice(0, tm), ATTN_WIDTH, GMLP_WIDTH, sumsq_gm, ggo_ref)
        gates_done += gates_per_chain


def _mixer(x, q, kv, u, vg, sink, w_sp, b_sp, g_ao, g_go, w_out, g_ffn, seq):
    t = x.shape[0]
    tm = TM_MIX
    nblk = tm // BLOCK
    n_tiles = t // tm
    tiles_per_seq = seq // tm
    last_block = t // BLOCK - 1
    cur = lambda i: jnp.minimum(i, n_tiles - 1)
    row = lambda i: (cur(i), 0)
    lag = lambda i: (jnp.maximum(i - 1, 0), 0)
    return pl.pallas_call(
        functools.partial(_mixer_kernel, tiles_per_seq, n_tiles),
        grid=(n_tiles + 1,),
        in_specs=[
            pl.BlockSpec((tm, D_MODEL), lag),
            pl.BlockSpec((tm, ATTN_WIDTH), row),
            pl.BlockSpec((BLOCK, 2 * KV_WIDTH),
                         lambda i: (jnp.maximum(cur(i) * nblk - 1, 0), 0)),
            pl.BlockSpec((tm, 2 * KV_WIDTH), row),
            pl.BlockSpec((BLOCK, 2 * KV_WIDTH),
                         lambda i: (jnp.minimum((cur(i) + 1) * nblk, last_block), 0)),
            pl.BlockSpec((tm, GMLP_WIDTH), row),
            pl.BlockSpec((tm, GMLP_WIDTH), row),
            pl.BlockSpec(memory_space=pltpu.SMEM),
            _const_spec((N_GMLP_HEADS, BLOCK, BLOCK)),
            _const_spec((N_GMLP_HEADS, BLOCK, BLOCK)),
            _const_spec((1, ATTN_WIDTH)),
            _const_spec((1, GMLP_WIDTH)),
            _const_spec((D_MODEL, D_MODEL)),
            _const_spec((1, D_MODEL)),
        ],
        out_specs=[pl.BlockSpec((tm, D_MODEL), lag), pl.BlockSpec((tm, D_MODEL), lag)],
        out_shape=[jax.ShapeDtypeStruct((t, D_MODEL), F32),
                   jax.ShapeDtypeStruct((t, D_MODEL), BF16)],
        scratch_shapes=[
            pltpu.VMEM((tm + 2 * BLOCK, 2 * KV_WIDTH), BF16),
            pltpu.VMEM((tm, D_MODEL), F32),
            pltpu.VMEM((2, tm, D_MODEL), BF16),
        ],
        compiler_params=pltpu.CompilerParams(
            dimension_semantics=("arbitrary",),
            vmem_limit_bytes=V7X_VMEM_LIMIT_BYTES),
        name="mixer",
    )(x, q, kv, kv, kv, u, vg, sink, w_sp, b_sp, g_ao, g_go, w_out, g_ffn)


def _ffn_kernel(xg_ref, x_ref, wup_ref, wdn_ref, o_ref):
    f = pl.program_id(1)
    n_f = pl.num_programs(1)
    n_sub = wup_ref.shape[1] // TF_SUB

    def partial_sum():
        def act(j):
            up = jnp.dot(xg_ref[...], wup_ref[:, j * TF_SUB:(j + 1) * TF_SUB],
                         preferred_element_type=F32)
            return jnp.square(jnp.maximum(up, 0.0)).astype(BF16)

        acc = None
        a_next = act(0)
        for j in range(n_sub):
            a_cur = a_next
            if j + 1 < n_sub:
                a_next = act(j + 1)
            d = jnp.dot(a_cur, wdn_ref[j * TF_SUB:(j + 1) * TF_SUB, :],
                        preferred_element_type=F32)
            acc = d if acc is None else acc + d
        return acc

    @pl.when(f == 0)
    def _():
        o_ref[...] = partial_sum()

    @pl.when((f > 0) & (f < n_f - 1))
    def _():
        o_ref[...] += partial_sum()

    @pl.when(f == n_f - 1)
    def _():
        x = x_ref[...]
        r2 = 1.0 / (jnp.mean(x * x, axis=-1, keepdims=True) + EPS)
        o_ref[...] = x + r2 * (o_ref[...] + partial_sum())


def _ffn(xg, x, w_up, w_down):
    t = x.shape[0]
    tm, tf = TM_FFN, TF_FFN
    return pl.pallas_call(
        _ffn_kernel,
        grid=(t // tm, D_FF // tf),
        in_specs=[
            pl.BlockSpec((tm, D_MODEL), lambda i, f: (i, 0)),
            pl.BlockSpec((tm, D_MODEL), lambda i, f: (i, 0)),
            pl.BlockSpec((D_MODEL, tf), lambda i, f: (0, f)),
            pl.BlockSpec((tf, D_MODEL), lambda i, f: (f, 0)),
        ],
        out_specs=pl.BlockSpec((tm, D_MODEL), lambda i, f: (i, 0)),
        out_shape=jax.ShapeDtypeStruct((t, D_MODEL), F32),
        compiler_params=pltpu.CompilerParams(
            dimension_semantics=("arbitrary", "arbitrary"),
            vmem_limit_bytes=V7X_VMEM_LIMIT_FFN_BYTES),
        name="ffn",
    )(xg, x, w_up, w_down)


def _rope_tables(seq):
    pos = jnp.arange(seq, dtype=F32)
    inv_freq = ROPE_THETA ** (-jnp.arange(0, ROT_DIM, 2, dtype=F32) / ROT_DIM)
    ang = pos[:, None] * inv_freq[None, :]
    cos, sin = jnp.cos(ang), jnp.sin(ang)
    half = ROT_DIM // 2
    pad = HEAD_DIM - ROT_DIM
    cos_t = jnp.concatenate([cos, cos, jnp.ones((seq, pad), F32)], axis=1)
    sina_t = jnp.concatenate([-sin, jnp.zeros((seq, HEAD_DIM - half), F32)], axis=1)
    sinb_t = jnp.concatenate([jnp.zeros((seq, half), F32), sin, jnp.zeros((seq, pad), F32)], axis=1)
    return cos_t, sina_t, sinb_t


def kernel(x_prompt, x_sample, g_mix, w_in, g_q, g_k, sink, g_v_ln, b_v_ln, w_spatial, b_spatial, g_attn_out, g_gmlp_out, w_out, g_ffn, w_up, w_down):
    streams = [x_prompt, x_sample]
    for l in range(g_mix.shape[0]):
        w_in_l = w_in[l].astype(BF16)
        w_sp_l = w_spatial[l].astype(BF16)
        b_sp_l = jnp.broadcast_to(b_spatial[l][:, :, None], (N_GMLP_HEADS, BLOCK, BLOCK))
        w_out_l = w_out[l].astype(BF16)
        mlp_weights = [w_up[l], w_down[l]]
        proj_out, mlp_weights_bf16 = [], []
        for x, w_cast in zip(streams, mlp_weights):
            seq = x.shape[1]
            *qkuv, w_bf16 = _in_proj(
                x.reshape(-1, D_MODEL), g_mix[l][None], w_in_l, g_q[l][None], g_k[l][None],
                g_v_ln[l][None], b_v_ln[l][None], _rope_tables(seq), seq, w_cast)
            proj_out.append(qkuv)
            mlp_weights_bf16.append(w_bf16)
        mixed = [
            _mixer(x.reshape(-1, D_MODEL), *qkuv, sink[l], w_sp_l, b_sp_l, g_attn_out[l][None],
                   g_gmlp_out[l][None], w_out_l, g_ffn[l][None], x.shape[1])
            for x, qkuv in zip(streams, proj_out)]
        streams = [
            _ffn(xg, x1, *mlp_weights_bf16).reshape(x.shape)
            for x, (x1, xg) in zip(streams, mixed)]
    return tuple(streams)
```

```python
import functools
import math

import jax
import jax.numpy as jnp
import numpy as np
from jax import lax
from jax.experimental import pallas as pl
from jax.experimental.pallas import tpu as pltpu

D_MODEL = 2048
HEAD_DIM = 128
N_ATTN_HEADS = 8
N_KV_HEADS = 2
GQA_GROUP = N_ATTN_HEADS // N_KV_HEADS
ATTN_WIDTH = N_ATTN_HEADS * HEAD_DIM
KV_WIDTH = N_KV_HEADS * HEAD_DIM
N_GMLP_HEADS = 8
GMLP_WIDTH = N_GMLP_HEADS * HEAD_DIM
IN_WIDTH = ATTN_WIDTH + 2 * KV_WIDTH + 2 * GMLP_WIDTH
BLOCK = 128
ROPE_THETA = 500000.0
ROT_DIM = HEAD_DIM // 4
D_FF = 4 * D_MODEL
EPS = 1e-6
NEG_INF = -1e30
LOG2_E = math.log2(math.e)

Q_OFF = 0
KV_OFF = ATTN_WIDTH
U_OFF = KV_OFF + 2 * KV_WIDTH
VG_OFF = U_OFF + GMLP_WIDTH

TM_PROJ = 512
ROW_PARTS_PROJ = 4
TM_MIX = 512
HEADS_PER_CHAIN = 2
OUT_SLAB = 256
GATES_PER_SLAB = 2
TM_FFN = 512
TF_FFN = 2048
TF_SUB = 512

V7X_VMEM_LIMIT_BYTES = 56 * 1024 * 1024
V7X_VMEM_LIMIT_FFN_BYTES = 60 * 1024 * 1024

F32 = jnp.float32
BF16 = jnp.bfloat16


def _rms_scale(x):
    return lax.rsqrt(jnp.mean(x * x, axis=-1, keepdims=True) + EPS)


def _const_spec(shape):
    nd = len(shape)
    return pl.BlockSpec(shape, lambda *_: (0,) * nd, pipeline_mode=pl.Buffered(1))


def _in_proj_kernel(x_ref, gmix_ref, w_ref, gq_ref, gk_ref, gln_ref, bln_ref,
                    cos_ref, sina_ref, sinb_ref, wcast_a_ref, wcast_b_ref,
                    q_ref, kv_ref, u_ref, vg_ref, wcast_a_out_ref, wcast_b_out_ref):
    tm = x_ref.shape[0]
    rows_per_part = tm // ROW_PARTS_PROJ

    wcast_a_out_ref[...] = wcast_a_ref[...].astype(BF16)
    wcast_b_out_ref[...] = wcast_b_ref[...].astype(BF16)

    def normed(r0):
        x = x_ref[r0:r0 + rows_per_part, :]
        return (x * _rms_scale(x) * gmix_ref[...]).astype(BF16)

    def norm_rope(r0, xh, g):
        rows = slice(r0, r0 + rows_per_part)
        y = xh * _rms_scale(xh) * g
        return (y * cos_ref[rows, :]
                + pltpu.roll(y, HEAD_DIM - ROT_DIM // 2, 1) * sina_ref[rows, :]
                + pltpu.roll(y, ROT_DIM // 2, 1) * sinb_ref[rows, :])

    def vg_epilogue(r0, p):
        p = jax.nn.gelu(p)
        mu = jnp.mean(p, axis=-1, keepdims=True)
        pc = p - mu
        y = pc * lax.rsqrt(jnp.mean(pc * pc, axis=-1, keepdims=True) + EPS)
        vg_ref[r0:r0 + rows_per_part, :] = (y * gln_ref[...] + bln_ref[...]).astype(BF16)

    def u_epilogue(c0, r0, p):
        u_ref[r0:r0 + rows_per_part, c0:c0 + p.shape[1]] = jax.nn.gelu(p).astype(BF16)

    def q_epilogue(c0, r0, p):
        g = gq_ref[...]
        for lo in range(0, p.shape[1], HEAD_DIM):
            q_ref[r0:r0 + rows_per_part, c0 + lo:c0 + lo + HEAD_DIM] = norm_rope(
                r0, p[:, lo:lo + HEAD_DIM], g).astype(BF16)

    def kv_epilogue(r0, p):
        g = gk_ref[...]
        for lo in range(0, KV_WIDTH, HEAD_DIM):
            kv_ref[r0:r0 + rows_per_part, lo:lo + HEAD_DIM] = norm_rope(
                r0, p[:, lo:lo + HEAD_DIM], g).astype(BF16)
        kv_ref[r0:r0 + rows_per_part, KV_WIDTH:] = p[:, KV_WIDTH:].astype(BF16)

    half_q, half_u = ATTN_WIDTH // 2, GMLP_WIDTH // 2
    groups = [
        (Q_OFF, half_q, functools.partial(q_epilogue, 0)),
        (Q_OFF + half_q, half_q, functools.partial(q_epilogue, half_q)),
        (KV_OFF, 2 * KV_WIDTH, kv_epilogue),
        (U_OFF, half_u, functools.partial(u_epilogue, 0)),
        (U_OFF + half_u, half_u, functools.partial(u_epilogue, half_u)),
        (VG_OFF, GMLP_WIDTH, vg_epilogue),
    ]
    stages = [(part, g) for part in range(ROW_PARTS_PROJ) for g in range(len(groups))]
    h = {0: normed(0)}

    def proj(stage):
        part, g = stage
        col, width, _ = groups[g]
        return jnp.dot(h[part], w_ref[:, col:col + width], preferred_element_type=F32)

    p_next = proj(stages[0])
    for k, (part, g) in enumerate(stages):
        p_cur = p_next
        if g == 0 and part + 1 < ROW_PARTS_PROJ:
            h[part + 1] = normed((part + 1) * rows_per_part)
        if k + 1 < len(stages):
            p_next = proj(stages[k + 1])
        groups[g][2](part * rows_per_part, p_cur)


def _in_proj(x, g_mix, w_in, g_q, g_k, g_ln, b_ln, rope, seq, w_cast):
    t = x.shape[0]
    tm = TM_PROJ
    n_tiles = t // tm
    tiles_per_seq = seq // tm
    row = lambda i: (i, 0)
    cast_specs = [pl.BlockSpec((w.shape[0] // n_tiles, w.shape[1]), row) for w in w_cast]
    rope_spec = pl.BlockSpec((tm, HEAD_DIM), lambda i: (i % tiles_per_seq, 0))
    return pl.pallas_call(
        _in_proj_kernel,
        grid=(t // tm,),
        in_specs=[
            pl.BlockSpec((tm, D_MODEL), row),
            _const_spec((1, D_MODEL)),
            _const_spec((D_MODEL, IN_WIDTH)),
            _const_spec((1, HEAD_DIM)),
            _const_spec((1, HEAD_DIM)),
            _const_spec((1, GMLP_WIDTH)),
            _const_spec((1, GMLP_WIDTH)),
            rope_spec, rope_spec, rope_spec,
            *cast_specs,
        ],
        out_specs=[
            pl.BlockSpec((tm, ATTN_WIDTH), row),
            pl.BlockSpec((tm, 2 * KV_WIDTH), row),
            pl.BlockSpec((tm, GMLP_WIDTH), row),
            pl.BlockSpec((tm, GMLP_WIDTH), row),
            *cast_specs,
        ],
        out_shape=[
            jax.ShapeDtypeStruct((t, ATTN_WIDTH), BF16),
            jax.ShapeDtypeStruct((t, 2 * KV_WIDTH), BF16),
            jax.ShapeDtypeStruct((t, GMLP_WIDTH), BF16),
            jax.ShapeDtypeStruct((t, GMLP_WIDTH), BF16),
            *[jax.ShapeDtypeStruct(w.shape, BF16) for w in w_cast],
        ],
        compiler_params=pltpu.CompilerParams(
            dimension_semantics=("arbitrary",),
            vmem_limit_bytes=V7X_VMEM_LIMIT_BYTES),
        name="in_proj",
    )(x, g_mix, w_in, g_q, g_k, g_ln, b_ln, *rope, *w_cast)


def _mixer_kernel(tiles_per_seq, n_tiles, x_ref, q_ref, kvp_ref, kvc_ref, kvn_ref, u_ref,
                  vg_ref, sink_ref, wsp_ref, bsp_ref, gao_ref, ggo_ref, wout_ref, gffn_ref,
                  o_ref, xg_ref, kv_buf, mix_buf, mixn_cur, mixn_prev):
    tm = x_ref.shape[0]
    nblk = tm // BLOCK
    step = pl.program_id(0)
    tile_in_seq = jnp.minimum(step, n_tiles - 1) % tiles_per_seq
    is_first = tile_in_seq == 0
    is_last = tile_in_seq == tiles_per_seq - 1

    @pl.when(step == 0)
    def _():
        mixn_cur[...] = jnp.zeros(mixn_cur.shape, BF16)

    mixn_prev[...] = mixn_cur[...]

    kv_buf[0:BLOCK, :] = kvp_ref[...]
    kv_buf[BLOCK:BLOCK + tm, :] = kvc_ref[...]
    kv_buf[BLOCK + tm:, :] = kvn_ref[...]

    qi = lax.broadcasted_iota(jnp.int32, (BLOCK, BLOCK), 0)
    kj = lax.broadcasted_iota(jnp.int32, (BLOCK, BLOCK), 1)
    zero = jnp.zeros((BLOCK, BLOCK), F32)
    prev_bias = jnp.where(kj >= qi, 0.0, NEG_INF).astype(F32)
    next_bias = jnp.where(kj <= qi, 0.0, NEG_INF).astype(F32)
    scale = 1.0 / math.sqrt(HEAD_DIM)

    def band_bias(b):
        pb, nb_ = prev_bias, next_bias
        if b == 0:
            pb = jnp.where(is_first, NEG_INF, pb)
        if b == nblk - 1:
            nb_ = jnp.where(is_last, NEG_INF, nb_)
        return jnp.concatenate([pb, zero, nb_], axis=1)[None]

    def scores(b, h0):
        r0 = b * BLOCK
        kvh = h0 // GQA_GROUP
        qs = jnp.concatenate(
            [q_ref[r0:r0 + BLOCK, h * HEAD_DIM:(h + 1) * HEAD_DIM]
             for h in range(h0, h0 + HEADS_PER_CHAIN)], axis=0)
        kwin = kv_buf[r0:r0 + 3 * BLOCK, kvh * HEAD_DIM:(kvh + 1) * HEAD_DIM]
        return lax.dot_general(qs, kwin, (((1,), (1,)), ((), ())),
                               preferred_element_type=F32)

    def softmax_pv(b, h0, s):
        r0 = b * BLOCK
        kvh = h0 // GQA_GROUP
        vwin = kv_buf[r0:r0 + 3 * BLOCK,
                      KV_WIDTH + kvh * HEAD_DIM:KV_WIDTH + (kvh + 1) * HEAD_DIM]
        s = s.reshape(HEADS_PER_CHAIN, BLOCK, 3 * BLOCK) + band_bias(b)
        sink2 = jnp.concatenate(
            [jnp.full((1, 1, 1), sink_ref[h] * LOG2_E, F32)
             for h in range(h0, h0 + HEADS_PER_CHAIN)], axis=0)
        m2 = jnp.maximum(jnp.max(s, axis=-1, keepdims=True) * (scale * LOG2_E), sink2)
        p = jnp.exp2(s * (scale * LOG2_E) - m2)
        denom = jnp.sum(p, axis=-1, keepdims=True) + jnp.exp2(sink2 - m2)
        o = jnp.dot(p.reshape(HEADS_PER_CHAIN * BLOCK, 3 * BLOCK).astype(BF16), vwin,
                    preferred_element_type=F32)
        o = o.reshape(HEADS_PER_CHAIN, BLOCK, HEAD_DIM) * (1.0 / denom)
        sq = jnp.zeros((BLOCK, HEAD_DIM), F32)
        for g in range(HEADS_PER_CHAIN):
            c0 = (h0 + g) * HEAD_DIM
            mix_buf[r0:r0 + BLOCK, c0:c0 + HEAD_DIM] = o[g]
            sq = sq + jnp.square(o[g])
        return sq

    def spatial_gate(hd):
        c0 = hd * HEAD_DIM
        vh = jnp.concatenate(
            [vg_ref[c * BLOCK:(c + 1) * BLOCK, c0:c0 + HEAD_DIM] for c in range(nblk)], axis=1)
        mixed = jnp.dot(wsp_ref[hd], vh, preferred_element_type=F32)
        bias_h = bsp_ref[hd]
        sq = []
        for c in range(nblk):
            uu = u_ref[c * BLOCK:(c + 1) * BLOCK, c0:c0 + HEAD_DIM].astype(F32)
            gated = uu * (mixed[:, c * BLOCK:(c + 1) * BLOCK] + bias_h)
            mix_buf[c * BLOCK:(c + 1) * BLOCK, ATTN_WIDTH + c0:ATTN_WIDTH + c0 + HEAD_DIM] = gated
            sq.append(jnp.square(gated))
        return jnp.concatenate(sq, axis=0)

    def group_norm_to(rows, col0, width, sumsq, g_ref):
        r = lax.rsqrt(jnp.sum(sumsq, axis=-1, keepdims=True) * (1.0 / width) + EPS)
        mixn_cur[rows, col0:col0 + width] = (
            mix_buf[rows, col0:col0 + width] * r * g_ref[...]).astype(BF16)

    chains = [(b, h0) for b in range(nblk) for h0 in range(0, N_ATTN_HEADS, HEADS_PER_CHAIN)]
    chains_per_slab = len(chains) * OUT_SLAB // D_MODEL
    sumsq_gm = jnp.zeros((tm, HEAD_DIM), F32)
    sumsq_attn = jnp.zeros((BLOCK, HEAD_DIM), F32)
    gates_done = 0
    s_next = scores(*chains[0])
    for k, (b, h0) in enumerate(chains):
        s_cur = s_next
        if k + 1 < len(chains):
            s_next = scores(*chains[k + 1])
        if k % chains_per_slab == 0:
            c0 = (k // chains_per_slab) * OUT_SLAB
            x1 = x_ref[:, c0:c0 + OUT_SLAB] + jnp.dot(
                mixn_prev[...], wout_ref[:, c0:c0 + OUT_SLAB], preferred_element_type=F32)
            o_ref[:, c0:c0 + OUT_SLAB] = x1
            xg_ref[:, c0:c0 + OUT_SLAB] = (x1 * gffn_ref[:, c0:c0 + OUT_SLAB]).astype(BF16)

        sq = softmax_pv(b, h0, s_cur)
        sumsq_attn = sq if h0 == 0 else sumsq_attn + sq
        if h0 + HEADS_PER_CHAIN == N_ATTN_HEADS:
            group_norm_to(slice(b * BLOCK, (b + 1) * BLOCK), 0, ATTN_WIDTH, sumsq_attn, gao_ref)

        if gates_done < N_GMLP_HEADS and k % chains_per_slab == chains_per_slab - 1:
            for hd in range(gates_done, gates_done + GATES_PER_SLAB):
                sumsq_gm = sumsq_gm + spatial_gate(hd)
            gates_done += GATES_PER_SLAB
            if gates_done == N_GMLP_HEADS:
                group_norm_to(slice(0, tm), ATTN_WIDTH, GMLP_WIDTH, sumsq_gm, ggo_ref)


def _mixer(x, q, kv, u, vg, sink, w_sp, b_sp, g_ao, g_go, w_out, g_ffn, seq):
    t = x.shape[0]
    tm = TM_MIX
    nblk = tm // BLOCK
    n_tiles = t // tm
    tiles_per_seq = seq // tm
    last_block = t // BLOCK - 1
    cur = lambda i: jnp.minimum(i, n_tiles - 1)
    row = lambda i: (cur(i), 0)
    lag = lambda i: (jnp.maximum(i - 1, 0), 0)
    return pl.pallas_call(
        functools.partial(_mixer_kernel, tiles_per_seq, n_tiles),
        grid=(n_tiles + 1,),
        in_specs=[
            pl.BlockSpec((tm, D_MODEL), lag),
            pl.BlockSpec((tm, ATTN_WIDTH), row),
            pl.BlockSpec((BLOCK, 2 * KV_WIDTH),
                         lambda i: (jnp.maximum(cur(i) * nblk - 1, 0), 0)),
            pl.BlockSpec((tm, 2 * KV_WIDTH), row),
            pl.BlockSpec((BLOCK, 2 * KV_WIDTH),
                         lambda i: (jnp.minimum((cur(i) + 1) * nblk, last_block), 0)),
            pl.BlockSpec((tm, GMLP_WIDTH), row),
            pl.BlockSpec((tm, GMLP_WIDTH), row),
            pl.BlockSpec(memory_space=pltpu.SMEM),
            _const_spec((N_GMLP_HEADS, BLOCK, BLOCK)),
            _const_spec((N_GMLP_HEADS, BLOCK, BLOCK)),
            _const_spec((1, ATTN_WIDTH)),
            _const_spec((1, GMLP_WIDTH)),
            _const_spec((D_MODEL, D_MODEL)),
            _const_spec((1, D_MODEL)),
        ],
        out_specs=[pl.BlockSpec((tm, D_MODEL), lag), pl.BlockSpec((tm, D_MODEL), lag)],
        out_shape=[jax.ShapeDtypeStruct((t, D_MODEL), F32),
                   jax.ShapeDtypeStruct((t, D_MODEL), BF16)],
        scratch_shapes=[
            pltpu.VMEM((tm + 2 * BLOCK, 2 * KV_WIDTH), BF16),
            pltpu.VMEM((tm, D_MODEL), F32),
            pltpu.VMEM((tm, D_MODEL), BF16),
            pltpu.VMEM((tm, D_MODEL), BF16),
        ],
        compiler_params=pltpu.CompilerParams(
            dimension_semantics=("arbitrary",),
            vmem_limit_bytes=V7X_VMEM_LIMIT_BYTES),
        name="mixer",
    )(x, q, kv, kv, kv, u, vg, sink, w_sp, b_sp, g_ao, g_go, w_out, g_ffn)


def _ffn_kernel(xg_ref, x_ref, wup_ref, wdn_ref, o_ref):
    f = pl.program_id(1)
    n_f = pl.num_programs(1)
    n_sub = wup_ref.shape[1] // TF_SUB

    def partial_sum():
        def act(j):
            up = jnp.dot(xg_ref[...], wup_ref[:, j * TF_SUB:(j + 1) * TF_SUB],
                         preferred_element_type=F32)
            return jnp.square(jnp.maximum(up, 0.0)).astype(BF16)

        acc = None
        a_next = act(0)
        for j in range(n_sub):
            a_cur = a_next
            if j + 1 < n_sub:
                a_next = act(j + 1)
            d = jnp.dot(a_cur, wdn_ref[j * TF_SUB:(j + 1) * TF_SUB, :],
                        preferred_element_type=F32)
            acc = d if acc is None else acc + d
        return acc

    @pl.when(f == 0)
    def _():
        o_ref[...] = partial_sum()

    @pl.when((f > 0) & (f < n_f - 1))
    def _():
        o_ref[...] += partial_sum()

    @pl.when(f == n_f - 1)
    def _():
        x = x_ref[...]
        r2 = 1.0 / (jnp.mean(x * x, axis=-1, keepdims=True) + EPS)
        o_ref[...] = x + r2 * (o_ref[...] + partial_sum())


def _ffn(xg, x, w_up, w_down):
    t = x.shape[0]
    tm, tf = TM_FFN, TF_FFN
    return pl.pallas_call(
        _ffn_kernel,
        grid=(t // tm, D_FF // tf),
        in_specs=[
            pl.BlockSpec((tm, D_MODEL), lambda i, f: (i, 0)),
            pl.BlockSpec((tm, D_MODEL), lambda i, f: (i, 0)),
            pl.BlockSpec((D_MODEL, tf), lambda i, f: (0, f)),
            pl.BlockSpec((tf, D_MODEL), lambda i, f: (f, 0)),
        ],
        out_specs=pl.BlockSpec((tm, D_MODEL), lambda i, f: (i, 0)),
        out_shape=jax.ShapeDtypeStruct((t, D_MODEL), F32),
        compiler_params=pltpu.CompilerParams(
            dimension_semantics=("arbitrary", "arbitrary"),
            vmem_limit_bytes=V7X_VMEM_LIMIT_FFN_BYTES),
        name="ffn",
    )(xg, x, w_up, w_down)


def _rope_tables(seq):
    pos = np.arange(seq, dtype=np.float64)
    inv_freq = ROPE_THETA ** (-np.arange(0, ROT_DIM, 2, dtype=np.float64) / ROT_DIM)
    ang = pos[:, None] * inv_freq[None, :]
    cos, sin = np.cos(ang), np.sin(ang)
    half = ROT_DIM // 2
    pad = HEAD_DIM - ROT_DIM
    cos_t = np.concatenate([cos, cos, np.ones((seq, pad))], axis=1)
    sina_t = np.concatenate([-sin, np.zeros((seq, HEAD_DIM - half))], axis=1)
    sinb_t = np.concatenate([np.zeros((seq, half)), sin, np.zeros((seq, pad))], axis=1)
    return tuple(jnp.asarray(t, dtype=F32) for t in (cos_t, sina_t, sinb_t))


def kernel(x_prompt, x_sample, g_mix, w_in, g_q, g_k, sink, g_v_ln, b_v_ln, w_spatial, b_spatial, g_attn_out, g_gmlp_out, w_out, g_ffn, w_up, w_down):
    streams = [x_prompt, x_sample]
    for l in range(g_mix.shape[0]):
        w_in_l = w_in[l].astype(BF16)
        b_sp_l = jnp.broadcast_to(b_spatial[l][:, :, None], (N_GMLP_HEADS, BLOCK, BLOCK))
        cast_jobs = [(w_up[l], w_out[l]),
                     (w_down[l], w_spatial[l].reshape(N_GMLP_HEADS * BLOCK, BLOCK))]
        proj_out, cast_out = [], []
        for x, w_cast in zip(streams, cast_jobs):
            seq = x.shape[1]
            *qkuv, w_a, w_b = _in_proj(
                x.reshape(-1, D_MODEL), g_mix[l][None], w_in_l, g_q[l][None], g_k[l][None],
                g_v_ln[l][None], b_v_ln[l][None], _rope_tables(seq), seq, w_cast)
            proj_out.append(qkuv)
            cast_out.append((w_a, w_b))
        (w_up_l, w_out_l), (w_down_l, w_sp_l) = cast_out
        w_sp_l = w_sp_l.reshape(N_GMLP_HEADS, BLOCK, BLOCK)
        mixed = [
            _mixer(x.reshape(-1, D_MODEL), *qkuv, sink[l], w_sp_l, b_sp_l, g_attn_out[l][None],
                   g_gmlp_out[l][None], w_out_l, g_ffn[l][None], x.shape[1])
            for x, qkuv in zip(streams, proj_out)]
        streams = [
            _ffn(xg, x1, w_up_l, w_down_l).reshape(x.shape)
            for x, (x1, xg) in zip(streams, mixed)]
    return tuple(streams)
```

```python
import functools
import math

import jax
import jax.numpy as jnp
import numpy as np
from jax import lax
from jax.experimental import pallas as pl
from jax.experimental.pallas import tpu as pltpu

D_MODEL = 2048
HEAD_DIM = 128
N_ATTN_HEADS = 8
N_KV_HEADS = 2
GQA_GROUP = N_ATTN_HEADS // N_KV_HEADS
ATTN_WIDTH = N_ATTN_HEADS * HEAD_DIM
KV_WIDTH = N_KV_HEADS * HEAD_DIM
N_GMLP_HEADS = 8
GMLP_WIDTH = N_GMLP_HEADS * HEAD_DIM
IN_WIDTH = ATTN_WIDTH + 2 * KV_WIDTH + 2 * GMLP_WIDTH
BLOCK = 128
ROPE_THETA = 500000.0
ROT_DIM = HEAD_DIM // 4
D_FF = 4 * D_MODEL
EPS = 1e-6
NEG_INF = -1e30
LOG2_E = math.log2(math.e)

Q_OFF = 0
KV_OFF = ATTN_WIDTH
U_OFF = KV_OFF + 2 * KV_WIDTH
VG_OFF = U_OFF + GMLP_WIDTH

TM_PROJ = 512
ROW_PARTS_PROJ = 4
TM_MIX = 512
HEADS_PER_CHAIN = 2
OUT_SLAB = 256
GATES_PER_SLAB = 2
TM_FFN = 512
TF_FFN = 2048
TF_SUB = 512

V7X_VMEM_LIMIT_BYTES = 56 * 1024 * 1024
V7X_VMEM_LIMIT_FFN_BYTES = 60 * 1024 * 1024

F32 = jnp.float32
BF16 = jnp.bfloat16


def _rms_scale(x):
    return lax.rsqrt(jnp.mean(x * x, axis=-1, keepdims=True) + EPS)


def _const_spec(shape):
    nd = len(shape)
    return pl.BlockSpec(shape, lambda *_: (0,) * nd, pipeline_mode=pl.Buffered(1))


def _in_proj_kernel(x_ref, gmix_ref, w_ref, gq_ref, gk_ref, gln_ref, bln_ref,
                    cos_ref, sina_ref, sinb_ref, wcast_a_ref, wcast_b_ref,
                    q_ref, kv_ref, u_ref, vg_ref, wcast_a_out_ref, wcast_b_out_ref):
    tm = x_ref.shape[0]
    rows_per_part = tm // ROW_PARTS_PROJ

    wcast_a_out_ref[...] = wcast_a_ref[...].astype(BF16)
    wcast_b_out_ref[...] = wcast_b_ref[...].astype(BF16)

    def normed(r0):
        x = x_ref[r0:r0 + rows_per_part, :]
        return (x * _rms_scale(x) * gmix_ref[...]).astype(BF16)

    def norm_rope(r0, xh, g):
        rows = slice(r0, r0 + rows_per_part)
        y = xh * _rms_scale(xh) * g
        return (y * cos_ref[rows, :]
                + pltpu.roll(y, HEAD_DIM - ROT_DIM // 2, 1) * sina_ref[rows, :]
                + pltpu.roll(y, ROT_DIM // 2, 1) * sinb_ref[rows, :])

    def vg_epilogue(r0, p):
        p = jax.nn.gelu(p)
        mu = jnp.mean(p, axis=-1, keepdims=True)
        pc = p - mu
        y = pc * lax.rsqrt(jnp.mean(pc * pc, axis=-1, keepdims=True) + EPS)
        vg_ref[r0:r0 + rows_per_part, :] = (y * gln_ref[...] + bln_ref[...]).astype(BF16)

    def u_epilogue(c0, r0, p):
        u_ref[r0:r0 + rows_per_part, c0:c0 + p.shape[1]] = jax.nn.gelu(p).astype(BF16)

    def q_epilogue(c0, r0, p):
        g = gq_ref[...]
        for lo in range(0, p.shape[1], HEAD_DIM):
            q_ref[r0:r0 + rows_per_part, c0 + lo:c0 + lo + HEAD_DIM] = norm_rope(
                r0, p[:, lo:lo + HEAD_DIM], g).astype(BF16)

    def k_epilogue(r0, p):
        g = gk_ref[...]
        for lo in range(0, KV_WIDTH, HEAD_DIM):
            kv_ref[r0:r0 + rows_per_part, lo:lo + HEAD_DIM] = norm_rope(
                r0, p[:, lo:lo + HEAD_DIM], g).astype(BF16)

    def v_epilogue(r0, p):
        kv_ref[r0:r0 + rows_per_part, KV_WIDTH:] = p.astype(BF16)

    half_q, half_u = ATTN_WIDTH // 2, GMLP_WIDTH // 2
    groups = [
        (VG_OFF, GMLP_WIDTH, vg_epilogue),
        (Q_OFF, half_q, functools.partial(q_epilogue, 0)),
        (Q_OFF + half_q, half_q, functools.partial(q_epilogue, half_q)),
        (KV_OFF, KV_WIDTH, k_epilogue),
        (U_OFF, half_u, functools.partial(u_epilogue, 0)),
        (U_OFF + half_u, half_u, functools.partial(u_epilogue, half_u)),
        (KV_OFF + KV_WIDTH, KV_WIDTH, v_epilogue),
    ]
    stages = [(part, g) for part in range(ROW_PARTS_PROJ) for g in range(len(groups))]
    h = {0: normed(0)}

    def proj(stage):
        part, g = stage
        col, width, _ = groups[g]
        return jnp.dot(h[part], w_ref[:, col:col + width], preferred_element_type=F32)

    p_next = proj(stages[0])
    for k, (part, g) in enumerate(stages):
        p_cur = p_next
        if g == 0 and part + 1 < ROW_PARTS_PROJ:
            h[part + 1] = normed((part + 1) * rows_per_part)
        if k + 1 < len(stages):
            p_next = proj(stages[k + 1])
        groups[g][2](part * rows_per_part, p_cur)


def _in_proj(x, g_mix, w_in, g_q, g_k, g_ln, b_ln, rope, seq, w_cast):
    t = x.shape[0]
    tm = TM_PROJ
    n_tiles = t // tm
    tiles_per_seq = seq // tm
    row = lambda i: (i, 0)
    cast_specs = [pl.BlockSpec((w.shape[0] // n_tiles, w.shape[1]), row) for w in w_cast]
    rope_spec = pl.BlockSpec((tm, HEAD_DIM), lambda i: (i % tiles_per_seq, 0))
    return pl.pallas_call(
        _in_proj_kernel,
        grid=(t // tm,),
        in_specs=[
            pl.BlockSpec((tm, D_MODEL), row),
            _const_spec((1, D_MODEL)),
            _const_spec((D_MODEL, IN_WIDTH)),
            _const_spec((1, HEAD_DIM)),
            _const_spec((1, HEAD_DIM)),
            _const_spec((1, GMLP_WIDTH)),
            _const_spec((1, GMLP_WIDTH)),
            rope_spec, rope_spec, rope_spec,
            *cast_specs,
        ],
        out_specs=[
            pl.BlockSpec((tm, ATTN_WIDTH), row),
            pl.BlockSpec((tm, 2 * KV_WIDTH), row),
            pl.BlockSpec((tm, GMLP_WIDTH), row),
            pl.BlockSpec((tm, GMLP_WIDTH), row),
            *cast_specs,
        ],
        out_shape=[
            jax.ShapeDtypeStruct((t, ATTN_WIDTH), BF16),
            jax.ShapeDtypeStruct((t, 2 * KV_WIDTH), BF16),
            jax.ShapeDtypeStruct((t, GMLP_WIDTH), BF16),
            jax.ShapeDtypeStruct((t, GMLP_WIDTH), BF16),
            *[jax.ShapeDtypeStruct(w.shape, BF16) for w in w_cast],
        ],
        compiler_params=pltpu.CompilerParams(
            dimension_semantics=("arbitrary",),
            vmem_limit_bytes=V7X_VMEM_LIMIT_BYTES),
        name="in_proj",
    )(x, g_mix, w_in, g_q, g_k, g_ln, b_ln, *rope, *w_cast)


def _mixer_kernel(tiles_per_seq, n_tiles, x_ref, q_ref, kvp_ref, kvc_ref, kvn_ref, u_ref,
                  vg_ref, sink_ref, wsp_ref, bsp_ref, gao_ref, ggo_ref, wout_ref, gffn_ref,
                  o_ref, xg_ref, kv_buf, mix_buf, mixn_cur, mixn_prev):
    tm = x_ref.shape[0]
    nblk = tm // BLOCK
    step = pl.program_id(0)
    tile_in_seq = step % tiles_per_seq
    is_first = tile_in_seq == 0
    is_last = tile_in_seq == tiles_per_seq - 1

    def fill_kv_buf():
        kv_buf[0:BLOCK, :] = kvp_ref[...]
        kv_buf[BLOCK:BLOCK + tm, :] = kvc_ref[...]
        kv_buf[BLOCK + tm:, :] = kvn_ref[...]

    scale = 1.0 / math.sqrt(HEAD_DIM)

    def band_bias(b):
        qi = lax.broadcasted_iota(jnp.int32, (BLOCK, BLOCK), 0)
        kj = lax.broadcasted_iota(jnp.int32, (BLOCK, BLOCK), 1)
        pb = jnp.where(kj >= qi, 0.0, NEG_INF).astype(F32)
        nb_ = jnp.where(kj <= qi, 0.0, NEG_INF).astype(F32)
        if b == 0:
            pb = jnp.where(is_first, NEG_INF, pb)
        if b == nblk - 1:
            nb_ = jnp.where(is_last, NEG_INF, nb_)
        zero = jnp.zeros((BLOCK, BLOCK), F32)
        return jnp.concatenate([pb, zero, nb_], axis=1)[None]

    def scores(b, h0):
        r0 = b * BLOCK
        kvh = h0 // GQA_GROUP
        qs = jnp.concatenate(
            [q_ref[r0:r0 + BLOCK, h * HEAD_DIM:(h + 1) * HEAD_DIM]
             for h in range(h0, h0 + HEADS_PER_CHAIN)], axis=0)
        kwin = kv_buf[r0:r0 + 3 * BLOCK, kvh * HEAD_DIM:(kvh + 1) * HEAD_DIM]
        return lax.dot_general(qs, kwin, (((1,), (1,)), ((), ())),
                               preferred_element_type=F32)

    def softmax_pv(b, h0, s):
        r0 = b * BLOCK
        kvh = h0 // GQA_GROUP
        vwin = kv_buf[r0:r0 + 3 * BLOCK,
                      KV_WIDTH + kvh * HEAD_DIM:KV_WIDTH + (kvh + 1) * HEAD_DIM]
        s = s.reshape(HEADS_PER_CHAIN, BLOCK, 3 * BLOCK) + band_bias(b)
        sink2 = jnp.concatenate(
            [jnp.full((1, 1, 1), sink_ref[h] * LOG2_E, F32)
             for h in range(h0, h0 + HEADS_PER_CHAIN)], axis=0)
        m2 = jnp.maximum(jnp.max(s, axis=-1, keepdims=True) * (scale * LOG2_E), sink2)
        p = jnp.exp2(s * (scale * LOG2_E) - m2)
        denom = jnp.sum(p, axis=-1, keepdims=True) + jnp.exp2(sink2 - m2)
        o = jnp.dot(p.reshape(HEADS_PER_CHAIN * BLOCK, 3 * BLOCK).astype(BF16), vwin,
                    preferred_element_type=F32)
        o = o.reshape(HEADS_PER_CHAIN, BLOCK, HEAD_DIM) * (1.0 / denom)
        sq = jnp.zeros((BLOCK, HEAD_DIM), F32)
        for g in range(HEADS_PER_CHAIN):
            c0 = (h0 + g) * HEAD_DIM
            mix_buf[r0:r0 + BLOCK, c0:c0 + HEAD_DIM] = o[g]
            sq = sq + jnp.square(o[g])
        return sq

    def spatial_gate(hd):
        c0 = hd * HEAD_DIM
        vh = jnp.concatenate(
            [vg_ref[c * BLOCK:(c + 1) * BLOCK, c0:c0 + HEAD_DIM] for c in range(nblk)], axis=1)
        mixed = jnp.dot(wsp_ref[hd], vh, preferred_element_type=F32)
        bias_h = bsp_ref[hd]
        sq = []
        for c in range(nblk):
            uu = u_ref[c * BLOCK:(c + 1) * BLOCK, c0:c0 + HEAD_DIM].astype(F32)
            gated = uu * (mixed[:, c * BLOCK:(c + 1) * BLOCK] + bias_h)
            mix_buf[c * BLOCK:(c + 1) * BLOCK, ATTN_WIDTH + c0:ATTN_WIDTH + c0 + HEAD_DIM] = gated
            sq.append(jnp.square(gated))
        return jnp.concatenate(sq, axis=0)

    def group_norm_to(rows, col0, width, sumsq, g_ref):
        r = lax.rsqrt(jnp.sum(sumsq, axis=-1, keepdims=True) * (1.0 / width) + EPS)
        mixn_cur[rows, col0:col0 + width] = (
            mix_buf[rows, col0:col0 + width] * r * g_ref[...]).astype(BF16)

    def project_slab(j):
        c0 = j * OUT_SLAB
        x1 = x_ref[:, c0:c0 + OUT_SLAB] + jnp.dot(
            mixn_prev[...], wout_ref[:, c0:c0 + OUT_SLAB], preferred_element_type=F32)
        o_ref[:, c0:c0 + OUT_SLAB] = x1
        xg_ref[:, c0:c0 + OUT_SLAB] = (x1 * gffn_ref[:, c0:c0 + OUT_SLAB]).astype(BF16)

    def run(do_mix, do_project):
        if do_project:
            mixn_prev[...] = mixn_cur[...]
        if not do_mix:
            for j in range(D_MODEL // OUT_SLAB):
                project_slab(j)
            return
        fill_kv_buf()
        chains = [(b, h0) for b in range(nblk)
                  for h0 in range(0, N_ATTN_HEADS, HEADS_PER_CHAIN)]
        chains_per_slab = len(chains) * OUT_SLAB // D_MODEL
        sumsq_gm = jnp.zeros((tm, HEAD_DIM), F32)
        sumsq_attn = jnp.zeros((BLOCK, HEAD_DIM), F32)
        gates_done = 0
        s_next = scores(*chains[0])
        for k, (b, h0) in enumerate(chains):
            s_cur = s_next
            if k + 1 < len(chains):
                s_next = scores(*chains[k + 1])
            if do_project and k % chains_per_slab == 0:
                project_slab(k // chains_per_slab)

            sq = softmax_pv(b, h0, s_cur)
            sumsq_attn = sq if h0 == 0 else sumsq_attn + sq
            if h0 + HEADS_PER_CHAIN == N_ATTN_HEADS:
                group_norm_to(slice(b * BLOCK, (b + 1) * BLOCK), 0, ATTN_WIDTH, sumsq_attn,
                              gao_ref)

            if gates_done < N_GMLP_HEADS and k % chains_per_slab == chains_per_slab - 1:
                for hd in range(gates_done, gates_done + GATES_PER_SLAB):
                    sumsq_gm = sumsq_gm + spatial_gate(hd)
                gates_done += GATES_PER_SLAB
                if gates_done == N_GMLP_HEADS:
                    group_norm_to(slice(0, tm), ATTN_WIDTH, GMLP_WIDTH, sumsq_gm, ggo_ref)

    pl.when(step == 0)(functools.partial(run, True, False))
    pl.when((step > 0) & (step < n_tiles))(functools.partial(run, True, True))
    pl.when(step == n_tiles)(functools.partial(run, False, True))


def _mixer(x, q, kv, u, vg, sink, w_sp, b_sp, g_ao, g_go, w_out, g_ffn, seq):
    t = x.shape[0]
    tm = TM_MIX
    nblk = tm // BLOCK
    n_tiles = t // tm
    tiles_per_seq = seq // tm
    last_block = t // BLOCK - 1
    cur = lambda i: jnp.minimum(i, n_tiles - 1)
    row = lambda i: (cur(i), 0)
    lag = lambda i: (jnp.maximum(i - 1, 0), 0)
    return pl.pallas_call(
        functools.partial(_mixer_kernel, tiles_per_seq, n_tiles),
        grid=(n_tiles + 1,),
        in_specs=[
            pl.BlockSpec((tm, D_MODEL), lag),
            pl.BlockSpec((tm, ATTN_WIDTH), row),
            pl.BlockSpec((BLOCK, 2 * KV_WIDTH),
                         lambda i: (jnp.maximum(cur(i) * nblk - 1, 0), 0)),
            pl.BlockSpec((tm, 2 * KV_WIDTH), row),
            pl.BlockSpec((BLOCK, 2 * KV_WIDTH),
                         lambda i: (jnp.minimum((cur(i) + 1) * nblk, last_block), 0)),
            pl.BlockSpec((tm, GMLP_WIDTH), row),
            pl.BlockSpec((tm, GMLP_WIDTH), row),
            pl.BlockSpec(memory_space=pltpu.SMEM),
            _const_spec((N_GMLP_HEADS, BLOCK, BLOCK)),
            _const_spec((N_GMLP_HEADS, BLOCK, BLOCK)),
            _const_spec((1, ATTN_WIDTH)),
            _const_spec((1, GMLP_WIDTH)),
            _const_spec((D_MODEL, D_MODEL)),
            _const_spec((1, D_MODEL)),
        ],
        out_specs=[pl.BlockSpec((tm, D_MODEL), lag), pl.BlockSpec((tm, D_MODEL), lag)],
        out_shape=[jax.ShapeDtypeStruct((t, D_MODEL), F32),
                   jax.ShapeDtypeStruct((t, D_MODEL), BF16)],
        scratch_shapes=[
            pltpu.VMEM((tm + 2 * BLOCK, 2 * KV_WIDTH), BF16),
            pltpu.VMEM((tm, D_MODEL), F32),
            pltpu.VMEM((tm, D_MODEL), BF16),
            pltpu.VMEM((tm, D_MODEL), BF16),
        ],
        compiler_params=pltpu.CompilerParams(
            dimension_semantics=("arbitrary",),
            vmem_limit_bytes=V7X_VMEM_LIMIT_BYTES),
        name="mixer",
    )(x, q, kv, kv, kv, u, vg, sink, w_sp, b_sp, g_ao, g_go, w_out, g_ffn)


def _ffn_kernel(xg_ref, x_ref, wup_ref, wdn_ref, o_ref):
    f = pl.program_id(1)
    n_f = pl.num_programs(1)
    n_sub = wup_ref.shape[1] // TF_SUB

    def partial_sum():
        def act(j):
            up = jnp.dot(xg_ref[...], wup_ref[:, j * TF_SUB:(j + 1) * TF_SUB],
                         preferred_element_type=F32)
            return jnp.square(jnp.maximum(up, 0.0)).astype(BF16)

        acc = None
        a_next = act(0)
        for j in range(n_sub):
            a_cur = a_next
            if j + 1 < n_sub:
                a_next = act(j + 1)
            d = jnp.dot(a_cur, wdn_ref[j * TF_SUB:(j + 1) * TF_SUB, :],
                        preferred_element_type=F32)
            acc = d if acc is None else acc + d
        return acc

    @pl.when(f == 0)
    def _():
        o_ref[...] = partial_sum()

    @pl.when((f > 0) & (f < n_f - 1))
    def _():
        o_ref[...] += partial_sum()

    @pl.when(f == n_f - 1)
    def _():
        x = x_ref[...]
        r2 = 1.0 / (jnp.mean(x * x, axis=-1, keepdims=True) + EPS)
        o_ref[...] = x + r2 * (o_ref[...] + partial_sum())


def _ffn(xg, x, w_up, w_down):
    t = x.shape[0]
    tm, tf = TM_FFN, TF_FFN
    return pl.pallas_call(
        _ffn_kernel,
        grid=(t // tm, D_FF // tf),
        in_specs=[
            pl.BlockSpec((tm, D_MODEL), lambda i, f: (i, 0)),
            pl.BlockSpec((tm, D_MODEL), lambda i, f: (i, 0)),
            pl.BlockSpec((D_MODEL, tf), lambda i, f: (0, f)),
            pl.BlockSpec((tf, D_MODEL), lambda i, f: (f, 0)),
        ],
        out_specs=pl.BlockSpec((tm, D_MODEL), lambda i, f: (i, 0)),
        out_shape=jax.ShapeDtypeStruct((t, D_MODEL), F32),
        compiler_params=pltpu.CompilerParams(
            dimension_semantics=("arbitrary", "arbitrary"),
            vmem_limit_bytes=V7X_VMEM_LIMIT_FFN_BYTES),
        name="ffn",
    )(xg, x, w_up, w_down)


def _rope_tables(seq):
    pos = np.arange(seq, dtype=np.float64)
    inv_freq = ROPE_THETA ** (-np.arange(0, ROT_DIM, 2, dtype=np.float64) / ROT_DIM)
    ang = pos[:, None] * inv_freq[None, :]
    cos, sin = np.cos(ang), np.sin(ang)
    half = ROT_DIM // 2
    pad = HEAD_DIM - ROT_DIM
    cos_t = np.concatenate([cos, cos, np.ones((seq, pad))], axis=1)
    sina_t = np.concatenate([-sin, np.zeros((seq, HEAD_DIM - half))], axis=1)
    sinb_t = np.concatenate([np.zeros((seq, half)), sin, np.zeros((seq, pad))], axis=1)
    return tuple(jnp.asarray(t, dtype=F32) for t in (cos_t, sina_t, sinb_t))


def kernel(x_prompt, x_sample, g_mix, w_in, g_q, g_k, sink, g_v_ln, b_v_ln, w_spatial, b_spatial, g_attn_out, g_gmlp_out, w_out, g_ffn, w_up, w_down):
    streams = [x_prompt, x_sample]
    for l in range(g_mix.shape[0]):
        w_in_l = w_in[l].astype(BF16)
        b_sp_l = jnp.broadcast_to(b_spatial[l][:, :, None], (N_GMLP_HEADS, BLOCK, BLOCK))
        cast_jobs = [(w_up[l], w_out[l]),
                     (w_down[l], w_spatial[l].reshape(N_GMLP_HEADS * BLOCK, BLOCK))]
        proj_out, cast_out = [], []
        for x, w_cast in zip(streams, cast_jobs):
            seq = x.shape[1]
            *qkuv, w_a, w_b = _in_proj(
                x.reshape(-1, D_MODEL), g_mix[l][None], w_in_l, g_q[l][None], g_k[l][None],
                g_v_ln[l][None], b_v_ln[l][None], _rope_tables(seq), seq, w_cast)
            proj_out.append(qkuv)
            cast_out.append((w_a, w_b))
        (w_up_l, w_out_l), (w_down_l, w_sp_l) = cast_out
        w_sp_l = w_sp_l.reshape(N_GMLP_HEADS, BLOCK, BLOCK)
        mixed = [
            _mixer(x.reshape(-1, D_MODEL), *qkuv, sink[l], w_sp_l, b_sp_l, g_attn_out[l][None],
                   g_gmlp_out[l][None], w_out_l, g_ffn[l][None], x.shape[1])
            for x, qkuv in zip(streams, proj_out)]
        streams = [
            _ffn(xg, x1, w_up_l, w_down_l).reshape(x.shape)
            for x, (x1, xg) in zip(streams, mixed)]
    return tuple(streams)
```

```python
import functools
import math

import jax
import jax.numpy as jnp
import numpy as np
from jax import lax
from jax.experimental import pallas as pl
from jax.experimental.pallas import tpu as pltpu

D_MODEL = 2048
HEAD_DIM = 128
N_ATTN_HEADS = 8
N_KV_HEADS = 2
GQA_GROUP = N_ATTN_HEADS // N_KV_HEADS
ATTN_WIDTH = N_ATTN_HEADS * HEAD_DIM
KV_WIDTH = N_KV_HEADS * HEAD_DIM
N_GMLP_HEADS = 8
GMLP_WIDTH = N_GMLP_HEADS * HEAD_DIM
IN_WIDTH = ATTN_WIDTH + 2 * KV_WIDTH + 2 * GMLP_WIDTH
BLOCK = 128
ROPE_THETA = 500000.0
ROT_DIM = HEAD_DIM // 4
D_FF = 4 * D_MODEL
EPS = 1e-6
NEG_INF = -1e30
LOG2_E = math.log2(math.e)

Q_OFF = 0
KV_OFF = ATTN_WIDTH
U_OFF = KV_OFF + 2 * KV_WIDTH
VG_OFF = U_OFF + GMLP_WIDTH

TM_PROJ = 512
ROW_PARTS_PROJ = 4
TM_MIX = 512
HEADS_PER_CHAIN = 2
OUT_SLAB = 256
GATES_PER_SLAB = 2
TM_FFN = 512
TF_FFN = 2048
TF_SUB = 512

V7X_VMEM_LIMIT_BYTES = 56 * 1024 * 1024
V7X_VMEM_LIMIT_FFN_BYTES = 60 * 1024 * 1024

F32 = jnp.float32
BF16 = jnp.bfloat16


def _rms_scale(x):
    return lax.rsqrt(jnp.mean(x * x, axis=-1, keepdims=True) + EPS)


def _const_spec(shape):
    nd = len(shape)
    return pl.BlockSpec(shape, lambda *_: (0,) * nd, pipeline_mode=pl.Buffered(1))


def _in_proj_kernel(x_ref, gmix_ref, w_ref, gq_ref, gk_ref, gln_ref, bln_ref,
                    cos_ref, sina_ref, sinb_ref, wcast_a_ref, wcast_b_ref,
                    q_ref, kv_ref, u_ref, vg_ref, wcast_a_out_ref, wcast_b_out_ref):
    tm = x_ref.shape[0]
    rows_per_part = tm // ROW_PARTS_PROJ

    wcast_a_out_ref[...] = wcast_a_ref[...].astype(BF16)
    wcast_b_out_ref[...] = wcast_b_ref[...].astype(BF16)

    def normed(r0):
        x = x_ref[r0:r0 + rows_per_part, :]
        return (x * _rms_scale(x) * gmix_ref[...]).astype(BF16)

    def norm_rope(r0, xh, g):
        rows = slice(r0, r0 + rows_per_part)
        y = xh * _rms_scale(xh) * g
        return (y * cos_ref[rows, :]
                + pltpu.roll(y, HEAD_DIM - ROT_DIM // 2, 1) * sina_ref[rows, :]
                + pltpu.roll(y, ROT_DIM // 2, 1) * sinb_ref[rows, :])

    def vg_epilogue(r0, p):
        p = jax.nn.gelu(p)
        mu = jnp.mean(p, axis=-1, keepdims=True)
        pc = p - mu
        y = pc * lax.rsqrt(jnp.mean(pc * pc, axis=-1, keepdims=True) + EPS)
        vg_ref[r0:r0 + rows_per_part, :] = (y * gln_ref[...] + bln_ref[...]).astype(BF16)

    def u_epilogue(c0, r0, p):
        u_ref[r0:r0 + rows_per_part, c0:c0 + p.shape[1]] = jax.nn.gelu(p).astype(BF16)

    def q_epilogue(c0, r0, p):
        g = gq_ref[...]
        for lo in range(0, p.shape[1], HEAD_DIM):
            q_ref[r0:r0 + rows_per_part, c0 + lo:c0 + lo + HEAD_DIM] = norm_rope(
                r0, p[:, lo:lo + HEAD_DIM], g).astype(BF16)

    def k_epilogue(r0, p):
        g = gk_ref[...]
        for lo in range(0, KV_WIDTH, HEAD_DIM):
            kv_ref[r0:r0 + rows_per_part, lo:lo + HEAD_DIM] = norm_rope(
                r0, p[:, lo:lo + HEAD_DIM], g).astype(BF16)

    def v_epilogue(r0, p):
        kv_ref[r0:r0 + rows_per_part, KV_WIDTH:] = p.astype(BF16)

    half_q, half_u = ATTN_WIDTH // 2, GMLP_WIDTH // 2
    groups = [
        (VG_OFF, GMLP_WIDTH, vg_epilogue),
        (Q_OFF, half_q, functools.partial(q_epilogue, 0)),
        (Q_OFF + half_q, half_q, functools.partial(q_epilogue, half_q)),
        (KV_OFF, KV_WIDTH, k_epilogue),
        (U_OFF, half_u, functools.partial(u_epilogue, 0)),
        (U_OFF + half_u, half_u, functools.partial(u_epilogue, half_u)),
        (KV_OFF + KV_WIDTH, KV_WIDTH, v_epilogue),
    ]
    stages = [(part, g) for part in range(ROW_PARTS_PROJ) for g in range(len(groups))]
    h = {0: normed(0)}

    def proj(stage):
        part, g = stage
        col, width, _ = groups[g]
        return jnp.dot(h[part], w_ref[:, col:col + width], preferred_element_type=F32)

    p_next = proj(stages[0])
    for k, (part, g) in enumerate(stages):
        p_cur = p_next
        if g == 0 and part + 1 < ROW_PARTS_PROJ:
            h[part + 1] = normed((part + 1) * rows_per_part)
        if k + 1 < len(stages):
            p_next = proj(stages[k + 1])
        groups[g][2](part * rows_per_part, p_cur)


def _in_proj(x, g_mix, w_in, g_q, g_k, g_ln, b_ln, rope, seq, w_cast):
    t = x.shape[0]
    tm = TM_PROJ
    n_tiles = t // tm
    tiles_per_seq = seq // tm
    row = lambda i: (i, 0)
    cast_specs = [pl.BlockSpec((w.shape[0] // n_tiles, w.shape[1]), row) for w in w_cast]
    rope_spec = pl.BlockSpec((tm, HEAD_DIM), lambda i: (i % tiles_per_seq, 0))
    return pl.pallas_call(
        _in_proj_kernel,
        grid=(t // tm,),
        in_specs=[
            pl.BlockSpec((tm, D_MODEL), row),
            _const_spec((1, D_MODEL)),
            _const_spec((D_MODEL, IN_WIDTH)),
            _const_spec((1, HEAD_DIM)),
            _const_spec((1, HEAD_DIM)),
            _const_spec((1, GMLP_WIDTH)),
            _const_spec((1, GMLP_WIDTH)),
            rope_spec, rope_spec, rope_spec,
            *cast_specs,
        ],
        out_specs=[
            pl.BlockSpec((tm, ATTN_WIDTH), row),
            pl.BlockSpec((tm, 2 * KV_WIDTH), row),
            pl.BlockSpec((tm, GMLP_WIDTH), row),
            pl.BlockSpec((tm, GMLP_WIDTH), row),
            *cast_specs,
        ],
        out_shape=[
            jax.ShapeDtypeStruct((t, ATTN_WIDTH), BF16),
            jax.ShapeDtypeStruct((t, 2 * KV_WIDTH), BF16),
            jax.ShapeDtypeStruct((t, GMLP_WIDTH), BF16),
            jax.ShapeDtypeStruct((t, GMLP_WIDTH), BF16),
            *[jax.ShapeDtypeStruct(w.shape, BF16) for w in w_cast],
        ],
        compiler_params=pltpu.CompilerParams(
            dimension_semantics=("arbitrary",),
            vmem_limit_bytes=V7X_VMEM_LIMIT_BYTES),
        name="in_proj",
    )(x, g_mix, w_in, g_q, g_k, g_ln, b_ln, *rope, *w_cast)


def _mixer_kernel(tiles_per_seq, n_tiles, *refs):
    *io_refs, mixn_a, mixn_b = refs
    step = pl.program_id(0)

    @pl.when(step == 0)
    def _():
        mixn_b[...] = jnp.zeros(mixn_b.shape, BF16)

    run = functools.partial(_mixer_step, tiles_per_seq, n_tiles, *io_refs)
    pl.when(step % 2 == 0)(functools.partial(run, mixn_a, mixn_b))
    pl.when(step % 2 == 1)(functools.partial(run, mixn_b, mixn_a))


def _mixer_step(tiles_per_seq, n_tiles, x_ref, q_ref, kvp_ref, kvc_ref, kvn_ref, u_ref,
                vg_ref, sink_ref, wsp_ref, bsp_ref, gao_ref, ggo_ref, wout_ref, gffn_ref,
                o_ref, xg_ref, kv_buf, mix_buf, mixn_cur, mixn_prev):
    tm = x_ref.shape[0]
    nblk = tm // BLOCK
    step = pl.program_id(0)
    tile_in_seq = jnp.minimum(step, n_tiles - 1) % tiles_per_seq
    is_first = tile_in_seq == 0
    is_last = tile_in_seq == tiles_per_seq - 1

    kv_buf[0:BLOCK, :] = kvp_ref[...]
    kv_buf[BLOCK:BLOCK + tm, :] = kvc_ref[...]
    kv_buf[BLOCK + tm:, :] = kvn_ref[...]

    qi = lax.broadcasted_iota(jnp.int32, (BLOCK, BLOCK), 0)
    kj = lax.broadcasted_iota(jnp.int32, (BLOCK, BLOCK), 1)
    zero = jnp.zeros((BLOCK, BLOCK), F32)
    prev_bias = jnp.where(kj >= qi, 0.0, NEG_INF).astype(F32)
    next_bias = jnp.where(kj <= qi, 0.0, NEG_INF).astype(F32)
    scale = 1.0 / math.sqrt(HEAD_DIM)

    def band_bias(b):
        pb, nb_ = prev_bias, next_bias
        if b == 0:
            pb = jnp.where(is_first, NEG_INF, pb)
        if b == nblk - 1:
            nb_ = jnp.where(is_last, NEG_INF, nb_)
        return jnp.concatenate([pb, zero, nb_], axis=1)[None]

    def scores(b, h0):
        r0 = b * BLOCK
        kvh = h0 // GQA_GROUP
        qs = jnp.concatenate(
            [q_ref[r0:r0 + BLOCK, h * HEAD_DIM:(h + 1) * HEAD_DIM]
             for h in range(h0, h0 + HEADS_PER_CHAIN)], axis=0)
        kwin = kv_buf[r0:r0 + 3 * BLOCK, kvh * HEAD_DIM:(kvh + 1) * HEAD_DIM]
        return lax.dot_general(qs, kwin, (((1,), (1,)), ((), ())),
                               preferred_element_type=F32)

    def softmax_pv(b, h0, s):
        r0 = b * BLOCK
        kvh = h0 // GQA_GROUP
        vwin = kv_buf[r0:r0 + 3 * BLOCK,
                      KV_WIDTH + kvh * HEAD_DIM:KV_WIDTH + (kvh + 1) * HEAD_DIM]
        s = s.reshape(HEADS_PER_CHAIN, BLOCK, 3 * BLOCK) + band_bias(b)
        sink2 = jnp.concatenate(
            [jnp.full((1, 1, 1), sink_ref[h] * LOG2_E, F32)
             for h in range(h0, h0 + HEADS_PER_CHAIN)], axis=0)
        m2 = jnp.maximum(jnp.max(s, axis=-1, keepdims=True) * (scale * LOG2_E), sink2)
        p = jnp.exp2(s * (scale * LOG2_E) - m2)
        denom = jnp.sum(p, axis=-1, keepdims=True) + jnp.exp2(sink2 - m2)
        o = jnp.dot(p.reshape(HEADS_PER_CHAIN * BLOCK, 3 * BLOCK).astype(BF16), vwin,
                    preferred_element_type=F32)
        o = o.reshape(HEADS_PER_CHAIN, BLOCK, HEAD_DIM) * (1.0 / denom)
        sq = jnp.zeros((BLOCK, HEAD_DIM), F32)
        for g in range(HEADS_PER_CHAIN):
            c0 = (h0 + g) * HEAD_DIM
            mix_buf[r0:r0 + BLOCK, c0:c0 + HEAD_DIM] = o[g]
            sq = sq + jnp.square(o[g])
        return sq

    def spatial_gate(hd):
        c0 = hd * HEAD_DIM
        vh = jnp.concatenate(
            [vg_ref[c * BLOCK:(c + 1) * BLOCK, c0:c0 + HEAD_DIM] for c in range(nblk)], axis=1)
        mixed = jnp.dot(wsp_ref[hd], vh, preferred_element_type=F32)
        bias_h = bsp_ref[hd]
        sq = []
        for c in range(nblk):
            uu = u_ref[c * BLOCK:(c + 1) * BLOCK, c0:c0 + HEAD_DIM].astype(F32)
            gated = uu * (mixed[:, c * BLOCK:(c + 1) * BLOCK] + bias_h)
            mix_buf[c * BLOCK:(c + 1) * BLOCK, ATTN_WIDTH + c0:ATTN_WIDTH + c0 + HEAD_DIM] = gated
            sq.append(jnp.square(gated))
        return jnp.concatenate(sq, axis=0)

    def group_norm_to(rows, col0, width, sumsq, g_ref):
        r = lax.rsqrt(jnp.sum(sumsq, axis=-1, keepdims=True) * (1.0 / width) + EPS)
        mixn_cur[rows, col0:col0 + width] = (
            mix_buf[rows, col0:col0 + width] * r * g_ref[...]).astype(BF16)

    chains = [(b, h0) for b in range(nblk) for h0 in range(0, N_ATTN_HEADS, HEADS_PER_CHAIN)]
    chains_per_slab = len(chains) * OUT_SLAB // D_MODEL
    sumsq_gm = jnp.zeros((tm, HEAD_DIM), F32)
    sumsq_attn = jnp.zeros((BLOCK, HEAD_DIM), F32)
    gates_done = 0
    s_next = scores(*chains[0])
    for k, (b, h0) in enumerate(chains):
        s_cur = s_next
        if k + 1 < len(chains):
            s_next = scores(*chains[k + 1])
        if k % chains_per_slab == 0:
            c0 = (k // chains_per_slab) * OUT_SLAB
            x1 = x_ref[:, c0:c0 + OUT_SLAB] + jnp.dot(
                mixn_prev[...], wout_ref[:, c0:c0 + OUT_SLAB], preferred_element_type=F32)
            o_ref[:, c0:c0 + OUT_SLAB] = x1
            xg_ref[:, c0:c0 + OUT_SLAB] = (x1 * gffn_ref[:, c0:c0 + OUT_SLAB]).astype(BF16)

        sq = softmax_pv(b, h0, s_cur)
        sumsq_attn = sq if h0 == 0 else sumsq_attn + sq
        if h0 + HEADS_PER_CHAIN == N_ATTN_HEADS:
            group_norm_to(slice(b * BLOCK, (b + 1) * BLOCK), 0, ATTN_WIDTH, sumsq_attn, gao_ref)

        if gates_done < N_GMLP_HEADS and k % chains_per_slab == chains_per_slab - 1:
            for hd in range(gates_done, gates_done + GATES_PER_SLAB):
                sumsq_gm = sumsq_gm + spatial_gate(hd)
            gates_done += GATES_PER_SLAB
            if gates_done == N_GMLP_HEADS:
                group_norm_to(slice(0, tm), ATTN_WIDTH, GMLP_WIDTH, sumsq_gm, ggo_ref)


def _mixer(x, q, kv, u, vg, sink, w_sp, b_sp, g_ao, g_go, w_out, g_ffn, seq):
    t = x.shape[0]
    tm = TM_MIX
    nblk = tm // BLOCK
    n_tiles = t // tm
    tiles_per_seq = seq // tm
    last_block = t // BLOCK - 1
    cur = lambda i: jnp.minimum(i, n_tiles - 1)
    row = lambda i: (cur(i), 0)
    lag = lambda i: (jnp.maximum(i - 1, 0), 0)
    return pl.pallas_call(
        functools.partial(_mixer_kernel, tiles_per_seq, n_tiles),
        grid=(n_tiles + 1,),
        in_specs=[
            pl.BlockSpec((tm, D_MODEL), lag),
            pl.BlockSpec((tm, ATTN_WIDTH), row),
            pl.BlockSpec((BLOCK, 2 * KV_WIDTH),
                         lambda i: (jnp.maximum(cur(i) * nblk - 1, 0), 0)),
            pl.BlockSpec((tm, 2 * KV_WIDTH), row),
            pl.BlockSpec((BLOCK, 2 * KV_WIDTH),
                         lambda i: (jnp.minimum((cur(i) + 1) * nblk, last_block), 0)),
            pl.BlockSpec((tm, GMLP_WIDTH), row),
            pl.BlockSpec((tm, GMLP_WIDTH), row),
            pl.BlockSpec(memory_space=pltpu.SMEM),
            _const_spec((N_GMLP_HEADS, BLOCK, BLOCK)),
            _const_spec((N_GMLP_HEADS, BLOCK, BLOCK)),
            _const_spec((1, ATTN_WIDTH)),
            _const_spec((1, GMLP_WIDTH)),
            _const_spec((D_MODEL, D_MODEL)),
            _const_spec((1, D_MODEL)),
        ],
        out_specs=[pl.BlockSpec((tm, D_MODEL), lag), pl.BlockSpec((tm, D_MODEL), lag)],
        out_shape=[jax.ShapeDtypeStruct((t, D_MODEL), F32),
                   jax.ShapeDtypeStruct((t, D_MODEL), BF16)],
        scratch_shapes=[
            pltpu.VMEM((tm + 2 * BLOCK, 2 * KV_WIDTH), BF16),
            pltpu.VMEM((tm, D_MODEL), F32),
            pltpu.VMEM((tm, D_MODEL), BF16),
            pltpu.VMEM((tm, D_MODEL), BF16),
        ],
        compiler_params=pltpu.CompilerParams(
            dimension_semantics=("arbitrary",),
            vmem_limit_bytes=V7X_VMEM_LIMIT_BYTES),
        name="mixer",
    )(x, q, kv, kv, kv, u, vg, sink, w_sp, b_sp, g_ao, g_go, w_out, g_ffn)


def _ffn_kernel(xg_ref, x_ref, wup_ref, wdn_ref, o_ref):
    f = pl.program_id(1)
    n_f = pl.num_programs(1)
    n_sub = wup_ref.shape[1] // TF_SUB

    def partial_sum():
        def act(j):
            up = jnp.dot(xg_ref[...], wup_ref[:, j * TF_SUB:(j + 1) * TF_SUB],
                         preferred_element_type=F32)
            return jnp.square(jnp.maximum(up, 0.0)).astype(BF16)

        acc = None
        a_next = act(0)
        for j in range(n_sub):
            a_cur = a_next
            if j + 1 < n_sub:
                a_next = act(j + 1)
            d = jnp.dot(a_cur, wdn_ref[j * TF_SUB:(j + 1) * TF_SUB, :],
                        preferred_element_type=F32)
            acc = d if acc is None else acc + d
        return acc

    @pl.when(f == 0)
    def _():
        o_ref[...] = partial_sum()

    @pl.when((f > 0) & (f < n_f - 1))
    def _():
        o_ref[...] += partial_sum()

    @pl.when(f == n_f - 1)
    def _():
        x = x_ref[...]
        r2 = 1.0 / (jnp.mean(x * x, axis=-1, keepdims=True) + EPS)
        o_ref[...] = x + r2 * (o_ref[...] + partial_sum())


def _ffn(xg, x, w_up, w_down):
    t = x.shape[0]
    tm, tf = TM_FFN, TF_FFN
    return pl.pallas_call(
        _ffn_kernel,
        grid=(t // tm, D_FF // tf),
        in_specs=[
            pl.BlockSpec((tm, D_MODEL), lambda i, f: (i, 0)),
            pl.BlockSpec((tm, D_MODEL), lambda i, f: (i, 0)),
            pl.BlockSpec((D_MODEL, tf), lambda i, f: (0, f)),
            pl.BlockSpec((tf, D_MODEL), lambda i, f: (f, 0)),
        ],
        out_specs=pl.BlockSpec((tm, D_MODEL), lambda i, f: (i, 0)),
        out_shape=jax.ShapeDtypeStruct((t, D_MODEL), F32),
        compiler_params=pltpu.CompilerParams(
            dimension_semantics=("arbitrary", "arbitrary"),
            vmem_limit_bytes=V7X_VMEM_LIMIT_FFN_BYTES),
        name="ffn",
    )(xg, x, w_up, w_down)


def _rope_tables(seq):
    pos = np.arange(seq, dtype=np.float64)
    inv_freq = ROPE_THETA ** (-np.arange(0, ROT_DIM, 2, dtype=np.float64) / ROT_DIM)
    ang = pos[:, None] * inv_freq[None, :]
    cos, sin = np.cos(ang), np.sin(ang)
    half = ROT_DIM // 2
    pad = HEAD_DIM - ROT_DIM
    cos_t = np.concatenate([cos, cos, np.ones((seq, pad))], axis=1)
    sina_t = np.concatenate([-sin, np.zeros((seq, HEAD_DIM - half))], axis=1)
    sinb_t = np.concatenate([np.zeros((seq, half)), sin, np.zeros((seq, pad))], axis=1)
    return tuple(jnp.asarray(t, dtype=F32) for t in (cos_t, sina_t, sinb_t))


def kernel(x_prompt, x_sample, g_mix, w_in, g_q, g_k, sink, g_v_ln, b_v_ln, w_spatial, b_spatial, g_attn_out, g_gmlp_out, w_out, g_ffn, w_up, w_down):
    streams = [x_prompt, x_sample]
    for l in range(g_mix.shape[0]):
        w_in_l = w_in[l].astype(BF16)
        b_sp_l = jnp.broadcast_to(b_spatial[l][:, :, None], (N_GMLP_HEADS, BLOCK, BLOCK))
        cast_jobs = [(w_up[l], w_out[l]),
                     (w_down[l], w_spatial[l].reshape(N_GMLP_HEADS * BLOCK, BLOCK))]
        proj_out, cast_out = [], []
        for x, w_cast in zip(streams, cast_jobs):
            seq = x.shape[1]
            *qkuv, w_a, w_b = _in_proj(
                x.reshape(-1, D_MODEL), g_mix[l][None], w_in_l, g_q[l][None], g_k[l][None],
                g_v_ln[l][None], b_v_ln[l][None], _rope_tables(seq), seq, w_cast)
            proj_out.append(qkuv)
            cast_out.append((w_a, w_b))
        (w_up_l, w_out_l), (w_down_l, w_sp_l) = cast_out
        w_sp_l = w_sp_l.reshape(N_GMLP_HEADS, BLOCK, BLOCK)
        mixed = [
            _mixer(x.reshape(-1, D_MODEL), *qkuv, sink[l], w_sp_l, b_sp_l, g_attn_out[l][None],
                   g_gmlp_out[l][None], w_out_l, g_ffn[l][None], x.shape[1])
            for x, qkuv in zip(streams, proj_out)]
        streams = [
            _ffn(xg, x1, w_up_l, w_down_l).reshape(x.shape)
            for x, (x1, xg) in zip(streams, mixed)]
    return tuple(streams)
```

```python
import functools
import math

import jax
import jax.numpy as jnp
import numpy as np
from jax import lax
from jax.experimental import pallas as pl
from jax.experimental.pallas import tpu as pltpu

D_MODEL = 2048
HEAD_DIM = 128
N_ATTN_HEADS = 8
N_KV_HEADS = 2
GQA_GROUP = N_ATTN_HEADS // N_KV_HEADS
ATTN_WIDTH = N_ATTN_HEADS * HEAD_DIM
KV_WIDTH = N_KV_HEADS * HEAD_DIM
N_GMLP_HEADS = 8
GMLP_WIDTH = N_GMLP_HEADS * HEAD_DIM
IN_WIDTH = ATTN_WIDTH + 2 * KV_WIDTH + 2 * GMLP_WIDTH
BLOCK = 128
ROPE_THETA = 500000.0
ROT_DIM = HEAD_DIM // 4
D_FF = 4 * D_MODEL
EPS = 1e-6
NEG_INF = -1e30
LOG2_E = math.log2(math.e)

Q_OFF = 0
KV_OFF = ATTN_WIDTH
U_OFF = KV_OFF + 2 * KV_WIDTH
VG_OFF = U_OFF + GMLP_WIDTH

TM_PROJ = 512
ROW_PARTS_PROJ = 4
TM_MIX = 512
HEADS_PER_CHAIN = 2
OUT_SLAB = 256
GATES_PER_SLAB = 2
TM_FFN = 512
TF_FFN = 2048
TF_SUB = 512

V7X_VMEM_LIMIT_BYTES = 56 * 1024 * 1024
V7X_VMEM_LIMIT_FFN_BYTES = 60 * 1024 * 1024

F32 = jnp.float32
BF16 = jnp.bfloat16


def _rms_scale(x):
    return lax.rsqrt(jnp.mean(x * x, axis=-1, keepdims=True) + EPS)


def _const_spec(shape):
    nd = len(shape)
    return pl.BlockSpec(shape, lambda *_: (0,) * nd, pipeline_mode=pl.Buffered(1))


def _in_proj_kernel(x_ref, gmix_ref, w_ref, gq_ref, gk_ref, gln_ref, bln_ref,
                    cos_ref, sina_ref, sinb_ref, wcast_a_ref, wcast_b_ref,
                    q_ref, kv_ref, u_ref, vg_ref, wcast_a_out_ref, wcast_b_out_ref):
    tm = x_ref.shape[0]
    rows_per_part = tm // ROW_PARTS_PROJ

    wcast_a_out_ref[...] = wcast_a_ref[...].astype(BF16)
    wcast_b_out_ref[...] = wcast_b_ref[...].astype(BF16)

    def normed(r0):
        x = x_ref[r0:r0 + rows_per_part, :]
        return (x * _rms_scale(x) * gmix_ref[...]).astype(BF16)

    def norm_rope(r0, xh, g):
        rows = slice(r0, r0 + rows_per_part)
        y = xh * _rms_scale(xh) * g
        return (y * cos_ref[rows, :]
                + pltpu.roll(y, HEAD_DIM - ROT_DIM // 2, 1) * sina_ref[rows, :]
                + pltpu.roll(y, ROT_DIM // 2, 1) * sinb_ref[rows, :])

    def vg_epilogue(r0, p):
        p = jax.nn.gelu(p)
        mu = jnp.mean(p, axis=-1, keepdims=True)
        pc = p - mu
        y = pc * lax.rsqrt(jnp.mean(pc * pc, axis=-1, keepdims=True) + EPS)
        vg_ref[r0:r0 + rows_per_part, :] = (y * gln_ref[...] + bln_ref[...]).astype(BF16)

    def u_epilogue(c0, r0, p):
        u_ref[r0:r0 + rows_per_part, c0:c0 + p.shape[1]] = jax.nn.gelu(p).astype(BF16)

    def q_epilogue(c0, r0, p):
        g = gq_ref[...]
        for lo in range(0, p.shape[1], HEAD_DIM):
            q_ref[r0:r0 + rows_per_part, c0 + lo:c0 + lo + HEAD_DIM] = norm_rope(
                r0, p[:, lo:lo + HEAD_DIM], g).astype(BF16)

    def k_epilogue(r0, p):
        g = gk_ref[...]
        for lo in range(0, KV_WIDTH, HEAD_DIM):
            kv_ref[r0:r0 + rows_per_part, lo:lo + HEAD_DIM] = norm_rope(
                r0, p[:, lo:lo + HEAD_DIM], g).astype(BF16)

    def v_epilogue(r0, p):
        kv_ref[r0:r0 + rows_per_part, KV_WIDTH:] = p.astype(BF16)

    half_q, half_u = ATTN_WIDTH // 2, GMLP_WIDTH // 2
    groups = [
        (VG_OFF, GMLP_WIDTH, vg_epilogue),
        (Q_OFF, half_q, functools.partial(q_epilogue, 0)),
        (Q_OFF + half_q, half_q, functools.partial(q_epilogue, half_q)),
        (KV_OFF, KV_WIDTH, k_epilogue),
        (U_OFF, half_u, functools.partial(u_epilogue, 0)),
        (U_OFF + half_u, half_u, functools.partial(u_epilogue, half_u)),
        (KV_OFF + KV_WIDTH, KV_WIDTH, v_epilogue),
    ]
    stages = [(part, g) for part in range(ROW_PARTS_PROJ) for g in range(len(groups))]
    h = {0: normed(0)}

    def proj(stage):
        part, g = stage
        col, width, _ = groups[g]
        return jnp.dot(h[part], w_ref[:, col:col + width], preferred_element_type=F32)

    p_next = proj(stages[0])
    for k, (part, g) in enumerate(stages):
        p_cur = p_next
        if g == 0 and part + 1 < ROW_PARTS_PROJ:
            h[part + 1] = normed((part + 1) * rows_per_part)
        if k + 1 < len(stages):
            p_next = proj(stages[k + 1])
        groups[g][2](part * rows_per_part, p_cur)


def _in_proj(x, g_mix, w_in, g_q, g_k, g_ln, b_ln, rope, seq, w_cast):
    t = x.shape[0]
    tm = TM_PROJ
    n_tiles = t // tm
    tiles_per_seq = seq // tm
    row = lambda i: (i, 0)
    cast_specs = [pl.BlockSpec((w.shape[0] // n_tiles, w.shape[1]), row) for w in w_cast]
    rope_spec = pl.BlockSpec((tm, HEAD_DIM), lambda i: (i % tiles_per_seq, 0))
    return pl.pallas_call(
        _in_proj_kernel,
        grid=(t // tm,),
        in_specs=[
            pl.BlockSpec((tm, D_MODEL), row),
            _const_spec((1, D_MODEL)),
            _const_spec((D_MODEL, IN_WIDTH)),
            _const_spec((1, HEAD_DIM)),
            _const_spec((1, HEAD_DIM)),
            _const_spec((1, GMLP_WIDTH)),
            _const_spec((1, GMLP_WIDTH)),
            rope_spec, rope_spec, rope_spec,
            *cast_specs,
        ],
        out_specs=[
            pl.BlockSpec((tm, ATTN_WIDTH), row),
            pl.BlockSpec((tm, 2 * KV_WIDTH), row),
            pl.BlockSpec((tm, GMLP_WIDTH), row),
            pl.BlockSpec((tm, GMLP_WIDTH), row),
            *cast_specs,
        ],
        out_shape=[
            jax.ShapeDtypeStruct((t, ATTN_WIDTH), BF16),
            jax.ShapeDtypeStruct((t, 2 * KV_WIDTH), BF16),
            jax.ShapeDtypeStruct((t, GMLP_WIDTH), BF16),
            jax.ShapeDtypeStruct((t, GMLP_WIDTH), BF16),
            *[jax.ShapeDtypeStruct(w.shape, BF16) for w in w_cast],
        ],
        compiler_params=pltpu.CompilerParams(
            dimension_semantics=("arbitrary",),
            vmem_limit_bytes=V7X_VMEM_LIMIT_BYTES),
        name="in_proj",
    )(x, g_mix, w_in, g_q, g_k, g_ln, b_ln, *rope, *w_cast)


def _mixer_kernel(tiles_per_seq, n_tiles, x_ref, q_ref, kvp_ref, kvc_ref, kvn_ref, u_ref,
                  vg_ref, sink_ref, wsp_ref, bsp_ref, gao_ref, ggo_ref, wout_ref, gffn_ref,
                  o_ref, xg_ref, kv_buf, mix_buf, mixn_cur, mixn_prev):
    tm = x_ref.shape[0]
    nblk = tm // BLOCK
    step = pl.program_id(0)
    tile_in_seq = jnp.minimum(step, n_tiles - 1) % tiles_per_seq
    is_first = tile_in_seq == 0
    is_last = tile_in_seq == tiles_per_seq - 1

    @pl.when(step == 0)
    def _():
        mixn_cur[...] = jnp.zeros(mixn_cur.shape, BF16)

    mixn_prev[...] = mixn_cur[...]

    kv_buf[0:BLOCK, :] = kvp_ref[...]
    kv_buf[BLOCK:BLOCK + tm, :] = kvc_ref[...]
    kv_buf[BLOCK + tm:, :] = kvn_ref[...]

    qi = lax.broadcasted_iota(jnp.int32, (BLOCK, BLOCK), 0)
    kj = lax.broadcasted_iota(jnp.int32, (BLOCK, BLOCK), 1)
    zero = jnp.zeros((BLOCK, BLOCK), F32)
    prev_bias = jnp.where(kj >= qi, 0.0, NEG_INF).astype(F32)
    next_bias = jnp.where(kj <= qi, 0.0, NEG_INF).astype(F32)
    scale = 1.0 / math.sqrt(HEAD_DIM)

    def band_bias(b):
        pb, nb_ = prev_bias, next_bias
        if b == 0:
            pb = jnp.where(is_first, NEG_INF, pb)
        if b == nblk - 1:
            nb_ = jnp.where(is_last, NEG_INF, nb_)
        return jnp.concatenate([pb, zero, nb_], axis=1)[None]

    def scores(b, h0):
        r0 = b * BLOCK
        kvh = h0 // GQA_GROUP
        qs = jnp.concatenate(
            [q_ref[r0:r0 + BLOCK, h * HEAD_DIM:(h + 1) * HEAD_DIM]
             for h in range(h0, h0 + HEADS_PER_CHAIN)], axis=0)
        kwin = kv_buf[r0:r0 + 3 * BLOCK, kvh * HEAD_DIM:(kvh + 1) * HEAD_DIM]
        return lax.dot_general(qs, kwin, (((1,), (1,)), ((), ())),
                               preferred_element_type=F32)

    def softmax_pv(b, h0, s):
        r0 = b * BLOCK
        kvh = h0 // GQA_GROUP
        vwin = kv_buf[r0:r0 + 3 * BLOCK,
                      KV_WIDTH + kvh * HEAD_DIM:KV_WIDTH + (kvh + 1) * HEAD_DIM]
        s = s.reshape(HEADS_PER_CHAIN, BLOCK, 3 * BLOCK) + band_bias(b)
        sink2 = jnp.concatenate(
            [jnp.full((1, 1, 1), sink_ref[h] * LOG2_E, F32)
             for h in range(h0, h0 + HEADS_PER_CHAIN)], axis=0)
        m2 = jnp.maximum(jnp.max(s, axis=-1, keepdims=True) * (scale * LOG2_E), sink2)
        p = jnp.exp2(s * (scale * LOG2_E) - m2)
        denom = jnp.sum(p, axis=-1, keepdims=True) + jnp.exp2(sink2 - m2)
        o = jnp.dot(p.reshape(HEADS_PER_CHAIN * BLOCK, 3 * BLOCK).astype(BF16), vwin,
                    preferred_element_type=F32)
        o = o.reshape(HEADS_PER_CHAIN, BLOCK, HEAD_DIM) * (1.0 / denom)
        sq = jnp.zeros((BLOCK, HEAD_DIM), F32)
        for g in range(HEADS_PER_CHAIN):
            c0 = (h0 + g) * HEAD_DIM
            mix_buf[r0:r0 + BLOCK, c0:c0 + HEAD_DIM] = o[g]
            sq = sq + jnp.square(o[g])
        return sq

    def spatial_gate(hd):
        c0 = hd * HEAD_DIM
        vh = jnp.concatenate(
            [vg_ref[c * BLOCK:(c + 1) * BLOCK, c0:c0 + HEAD_DIM] for c in range(nblk)], axis=1)
        mixed = jnp.dot(wsp_ref[hd], vh, preferred_element_type=F32)
        bias_h = bsp_ref[hd]
        sq = []
        for c in range(nblk):
            uu = u_ref[c * BLOCK:(c + 1) * BLOCK, c0:c0 + HEAD_DIM].astype(F32)
            gated = uu * (mixed[:, c * BLOCK:(c + 1) * BLOCK] + bias_h)
            mix_buf[c * BLOCK:(c + 1) * BLOCK, ATTN_WIDTH + c0:ATTN_WIDTH + c0 + HEAD_DIM] = gated
            sq.append(jnp.square(gated))
        return jnp.concatenate(sq, axis=0)

    def group_norm_to(rows, col0, width, sumsq, g_ref):
        r = lax.rsqrt(jnp.sum(sumsq, axis=-1, keepdims=True) * (1.0 / width) + EPS)
        mixn_cur[rows, col0:col0 + width] = (
            mix_buf[rows, col0:col0 + width] * r * g_ref[...]).astype(BF16)

    chains = [(b, h0) for b in range(nblk) for h0 in range(0, N_ATTN_HEADS, HEADS_PER_CHAIN)]
    chains_per_slab = len(chains) * OUT_SLAB // D_MODEL
    sumsq_gm = jnp.zeros((tm, HEAD_DIM), F32)
    sumsq_attn = jnp.zeros((BLOCK, HEAD_DIM), F32)
    gates_done = 0
    s_next = scores(*chains[0])
    for k, (b, h0) in enumerate(chains):
        s_cur = s_next
        if k + 1 < len(chains):
            s_next = scores(*chains[k + 1])
        if k % chains_per_slab == 0:
            c0 = (k // chains_per_slab) * OUT_SLAB
            x1 = x_ref[:, c0:c0 + OUT_SLAB] + jnp.dot(
                mixn_prev[...], wout_ref[:, c0:c0 + OUT_SLAB], preferred_element_type=F32)
            o_ref[:, c0:c0 + OUT_SLAB] = x1
            xg_ref[:, c0:c0 + OUT_SLAB] = (x1 * gffn_ref[:, c0:c0 + OUT_SLAB]).astype(BF16)

        sq = softmax_pv(b, h0, s_cur)
        sumsq_attn = sq if h0 == 0 else sumsq_attn + sq
        if h0 + HEADS_PER_CHAIN == N_ATTN_HEADS:
            group_norm_to(slice(b * BLOCK, (b + 1) * BLOCK), 0, ATTN_WIDTH, sumsq_attn, gao_ref)

        if gates_done < N_GMLP_HEADS and k % chains_per_slab == chains_per_slab - 1:
            for hd in range(gates_done, gates_done + GATES_PER_SLAB):
                sumsq_gm = sumsq_gm + spatial_gate(hd)
            gates_done += GATES_PER_SLAB
            if gates_done == N_GMLP_HEADS:
                group_norm_to(slice(0, tm), ATTN_WIDTH, GMLP_WIDTH, sumsq_gm, ggo_ref)


def _mixer(x, q, kv, u, vg, sink, w_sp, b_sp, g_ao, g_go, w_out, g_ffn, seq):
    t = x.shape[0]
    tm = TM_MIX
    nblk = tm // BLOCK
    n_tiles = t // tm
    tiles_per_seq = seq // tm
    last_block = t // BLOCK - 1
    cur = lambda i: jnp.minimum(i, n_tiles - 1)
    row = lambda i: (cur(i), 0)
    lag = lambda i: (jnp.maximum(i - 1, 0), 0)
    return pl.pallas_call(
        functools.partial(_mixer_kernel, tiles_per_seq, n_tiles),
        grid=(n_tiles + 1,),
        in_specs=[
            pl.BlockSpec((tm, D_MODEL), lag),
            pl.BlockSpec((tm, ATTN_WIDTH), row),
            pl.BlockSpec((BLOCK, 2 * KV_WIDTH),
                         lambda i: (jnp.maximum(cur(i) * nblk - 1, 0), 0)),
            pl.BlockSpec((tm, 2 * KV_WIDTH), row),
            pl.BlockSpec((BLOCK, 2 * KV_WIDTH),
                         lambda i: (jnp.minimum((cur(i) + 1) * nblk, last_block), 0)),
            pl.BlockSpec((tm, GMLP_WIDTH), row),
            pl.BlockSpec((tm, GMLP_WIDTH), row),
            pl.BlockSpec(memory_space=pltpu.SMEM),
            _const_spec((N_GMLP_HEADS, BLOCK, BLOCK)),
            _const_spec((N_GMLP_HEADS, BLOCK, BLOCK)),
            _const_spec((1, ATTN_WIDTH)),
            _const_spec((1, GMLP_WIDTH)),
            _const_spec((D_MODEL, D_MODEL)),
            _const_spec((1, D_MODEL)),
        ],
        out_specs=[pl.BlockSpec((tm, D_MODEL), lag), pl.BlockSpec((tm, D_MODEL), lag)],
        out_shape=[jax.ShapeDtypeStruct((t, D_MODEL), F32),
                   jax.ShapeDtypeStruct((t, D_MODEL), BF16)],
        scratch_shapes=[
            pltpu.VMEM((tm + 2 * BLOCK, 2 * KV_WIDTH), BF16),
            pltpu.VMEM((tm, D_MODEL), F32),
            pltpu.VMEM((tm, D_MODEL), BF16),
            pltpu.VMEM((tm, D_MODEL), BF16),
        ],
        compiler_params=pltpu.CompilerParams(
            dimension_semantics=("arbitrary",),
            vmem_limit_bytes=V7X_VMEM_LIMIT_BYTES),
        name="mixer",
    )(x, q, kv, kv, kv, u, vg, sink, w_sp, b_sp, g_ao, g_go, w_out, g_ffn)


def _ffn_kernel(xg_ref, x_ref, wup_ref, wdn_ref, o_ref):
    f = pl.program_id(1)
    n_f = pl.num_programs(1)
    n_sub = wup_ref.shape[1] // TF_SUB

    def partial_sum():
        def act(j):
            up = jnp.dot(xg_ref[...], wup_ref[:, j * TF_SUB:(j + 1) * TF_SUB],
                         preferred_element_type=F32)
            return jnp.square(jnp.maximum(up, 0.0)).astype(BF16)

        acc = None
        a_next = act(0)
        for j in range(n_sub):
            a_cur = a_next
            if j + 1 < n_sub:
                a_next = act(j + 1)
            d = jnp.dot(a_cur, wdn_ref[j * TF_SUB:(j + 1) * TF_SUB, :],
                        preferred_element_type=F32)
            acc = d if acc is None else acc + d
        return acc

    @pl.when(f == 0)
    def _():
        o_ref[...] = jnp.zeros(o_ref.shape, F32)

    @pl.when(f < n_f - 1)
    def _():
        o_ref[...] += partial_sum()

    @pl.when(f == n_f - 1)
    def _():
        x = x_ref[...]
        r2 = 1.0 / (jnp.mean(x * x, axis=-1, keepdims=True) + EPS)
        o_ref[...] = x + r2 * (o_ref[...] + partial_sum())


def _ffn(xg, x, w_up, w_down):
    t = x.shape[0]
    tm, tf = TM_FFN, TF_FFN
    return pl.pallas_call(
        _ffn_kernel,
        grid=(t // tm, D_FF // tf),
        in_specs=[
            pl.BlockSpec((tm, D_MODEL), lambda i, f: (i, 0)),
            pl.BlockSpec((tm, D_MODEL), lambda i, f: (i, 0)),
            pl.BlockSpec((D_MODEL, tf), lambda i, f: (0, f)),
            pl.BlockSpec((tf, D_MODEL), lambda i, f: (f, 0)),
        ],
        out_specs=pl.BlockSpec((tm, D_MODEL), lambda i, f: (i, 0)),
        out_shape=jax.ShapeDtypeStruct((t, D_MODEL), F32),
        compiler_params=pltpu.CompilerParams(
            dimension_semantics=("arbitrary", "arbitrary"),
            vmem_limit_bytes=V7X_VMEM_LIMIT_FFN_BYTES),
        name="ffn",
    )(xg, x, w_up, w_down)


def _rope_tables(seq):
    pos = np.arange(seq, dtype=np.float64)
    inv_freq = ROPE_THETA ** (-np.arange(0, ROT_DIM, 2, dtype=np.float64) / ROT_DIM)
    ang = pos[:, None] * inv_freq[None, :]
    cos, sin = np.cos(ang), np.sin(ang)
    half = ROT_DIM // 2
    pad = HEAD_DIM - ROT_DIM
    cos_t = np.concatenate([cos, cos, np.ones((seq, pad))], axis=1)
    sina_t = np.concatenate([-sin, np.zeros((seq, HEAD_DIM - half))], axis=1)
    sinb_t = np.concatenate([np.zeros((seq, half)), sin, np.zeros((seq, pad))], axis=1)
    return tuple(jnp.asarray(t, dtype=F32) for t in (cos_t, sina_t, sinb_t))


def kernel(x_prompt, x_sample, g_mix, w_in, g_q, g_k, sink, g_v_ln, b_v_ln, w_spatial, b_spatial, g_attn_out, g_gmlp_out, w_out, g_ffn, w_up, w_down):
    streams = [x_prompt, x_sample]
    for l in range(g_mix.shape[0]):
        w_in_l = w_in[l].astype(BF16)
        b_sp_l = jnp.broadcast_to(b_spatial[l][:, :, None], (N_GMLP_HEADS, BLOCK, BLOCK))
        cast_jobs = [(w_up[l], w_out[l]),
                     (w_down[l], w_spatial[l].reshape(N_GMLP_HEADS * BLOCK, BLOCK))]
        proj_out, cast_out = [], []
        for x, w_cast in zip(streams, cast_jobs):
            seq = x.shape[1]
            *qkuv, w_a, w_b = _in_proj(
                x.reshape(-1, D_MODEL), g_mix[l][None], w_in_l, g_q[l][None], g_k[l][None],
                g_v_ln[l][None], b_v_ln[l][None], _rope_tables(seq), seq, w_cast)
            proj_out.append(qkuv)
            cast_out.append((w_a, w_b))
        (w_up_l, w_out_l), (w_down_l, w_sp_l) = cast_out
        w_sp_l = w_sp_l.reshape(N_GMLP_HEADS, BLOCK, BLOCK)
        mixed = [
            _mixer(x.reshape(-1, D_MODEL), *qkuv, sink[l], w_sp_l, b_sp_l, g_attn_out[l][None],
                   g_gmlp_out[l][None], w_out_l, g_ffn[l][None], x.shape[1])
            for x, qkuv in zip(streams, proj_out)]
        streams = [
            _ffn(xg, x1, w_up_l, w_down_l).reshape(x.shape)
            for x, (x1, xg) in zip(streams, mixed)]
    return tuple(streams)
```

```python
import functools
import math

import jax
import jax.numpy as jnp
import numpy as np
from jax import lax
from jax.experimental import pallas as pl
from jax.experimental.pallas import tpu as pltpu

D_MODEL = 2048
HEAD_DIM = 128
N_ATTN_HEADS = 8
N_KV_HEADS = 2
GQA_GROUP = N_ATTN_HEADS // N_KV_HEADS
ATTN_WIDTH = N_ATTN_HEADS * HEAD_DIM
KV_WIDTH = N_KV_HEADS * HEAD_DIM
N_GMLP_HEADS = 8
GMLP_WIDTH = N_GMLP_HEADS * HEAD_DIM
IN_WIDTH = ATTN_WIDTH + 2 * KV_WIDTH + 2 * GMLP_WIDTH
BLOCK = 128
ROPE_THETA = 500000.0
ROT_DIM = HEAD_DIM // 4
D_FF = 4 * D_MODEL
EPS = 1e-6
NEG_INF = -1e30
LOG2_E = math.log2(math.e)

Q_OFF = 0
KV_OFF = ATTN_WIDTH
U_OFF = KV_OFF + 2 * KV_WIDTH
VG_OFF = U_OFF + GMLP_WIDTH

TM_PROJ = 512
ROW_PARTS_PROJ = 4
TM_MIX = 512
HEADS_PER_CHAIN = 2
OUT_SLAB = 256
GATES_PER_SLAB = 2
TM_FFN = 512
TF_FFN = 2048
TF_SUB = 1024

V7X_VMEM_LIMIT_BYTES = 56 * 1024 * 1024

F32 = jnp.float32
BF16 = jnp.bfloat16


def _rms_scale(x):
    return lax.rsqrt(jnp.mean(x * x, axis=-1, keepdims=True) + EPS)


def _const_spec(shape):
    nd = len(shape)
    return pl.BlockSpec(shape, lambda *_: (0,) * nd, pipeline_mode=pl.Buffered(1))


def _in_proj_kernel(x_ref, gmix_ref, w_ref, gq_ref, gk_ref, gln_ref, bln_ref,
                    cos_ref, sina_ref, sinb_ref, wcast_a_ref, wcast_b_ref,
                    q_ref, kv_ref, u_ref, vg_ref, wcast_a_out_ref, wcast_b_out_ref):
    tm = x_ref.shape[0]
    rows_per_part = tm // ROW_PARTS_PROJ

    wcast_a_out_ref[...] = wcast_a_ref[...].astype(BF16)
    wcast_b_out_ref[...] = wcast_b_ref[...].astype(BF16)

    def normed(r0):
        x = x_ref[r0:r0 + rows_per_part, :]
        return (x * _rms_scale(x) * gmix_ref[...]).astype(BF16)

    def norm_rope(r0, xh, g):
        rows = slice(r0, r0 + rows_per_part)
        y = xh * _rms_scale(xh) * g
        return (y * cos_ref[rows, :]
                + pltpu.roll(y, HEAD_DIM - ROT_DIM // 2, 1) * sina_ref[rows, :]
                + pltpu.roll(y, ROT_DIM // 2, 1) * sinb_ref[rows, :])

    def vg_epilogue(r0, p):
        p = jax.nn.gelu(p)
        mu = jnp.mean(p, axis=-1, keepdims=True)
        pc = p - mu
        y = pc * lax.rsqrt(jnp.mean(pc * pc, axis=-1, keepdims=True) + EPS)
        vg_ref[r0:r0 + rows_per_part, :] = (y * gln_ref[...] + bln_ref[...]).astype(BF16)

    def u_epilogue(c0, r0, p):
        u_ref[r0:r0 + rows_per_part, c0:c0 + p.shape[1]] = jax.nn.gelu(p).astype(BF16)

    def q_epilogue(c0, r0, p):
        g = gq_ref[...]
        for lo in range(0, p.shape[1], HEAD_DIM):
            q_ref[r0:r0 + rows_per_part, c0 + lo:c0 + lo + HEAD_DIM] = norm_rope(
                r0, p[:, lo:lo + HEAD_DIM], g).astype(BF16)

    def k_epilogue(r0, p):
        g = gk_ref[...]
        for lo in range(0, KV_WIDTH, HEAD_DIM):
            kv_ref[r0:r0 + rows_per_part, lo:lo + HEAD_DIM] = norm_rope(
                r0, p[:, lo:lo + HEAD_DIM], g).astype(BF16)

    def v_epilogue(r0, p):
        kv_ref[r0:r0 + rows_per_part, KV_WIDTH:] = p.astype(BF16)

    half_q, half_u = ATTN_WIDTH // 2, GMLP_WIDTH // 2
    groups = [
        (VG_OFF, GMLP_WIDTH, vg_epilogue),
        (Q_OFF, half_q, functools.partial(q_epilogue, 0)),
        (Q_OFF + half_q, half_q, functools.partial(q_epilogue, half_q)),
        (KV_OFF, KV_WIDTH, k_epilogue),
        (U_OFF, half_u, functools.partial(u_epilogue, 0)),
        (U_OFF + half_u, half_u, functools.partial(u_epilogue, half_u)),
        (KV_OFF + KV_WIDTH, KV_WIDTH, v_epilogue),
    ]
    stages = [(part, g) for part in range(ROW_PARTS_PROJ) for g in range(len(groups))]
    h = {0: normed(0)}

    def proj(stage):
        part, g = stage
        col, width, _ = groups[g]
        return jnp.dot(h[part], w_ref[:, col:col + width], preferred_element_type=F32)

    p_next = proj(stages[0])
    for k, (part, g) in enumerate(stages):
        p_cur = p_next
        if g == 0 and part + 1 < ROW_PARTS_PROJ:
            h[part + 1] = normed((part + 1) * rows_per_part)
        if k + 1 < len(stages):
            p_next = proj(stages[k + 1])
        groups[g][2](part * rows_per_part, p_cur)


def _in_proj(x, g_mix, w_in, g_q, g_k, g_ln, b_ln, rope, seq, w_cast):
    t = x.shape[0]
    tm = TM_PROJ
    n_tiles = t // tm
    tiles_per_seq = seq // tm
    row = lambda i: (i, 0)
    cast_specs = [pl.BlockSpec((w.shape[0] // n_tiles, w.shape[1]), row) for w in w_cast]
    rope_spec = pl.BlockSpec((tm, HEAD_DIM), lambda i: (i % tiles_per_seq, 0))
    return pl.pallas_call(
        _in_proj_kernel,
        grid=(t // tm,),
        in_specs=[
            pl.BlockSpec((tm, D_MODEL), row),
            _const_spec((1, D_MODEL)),
            _const_spec((D_MODEL, IN_WIDTH)),
            _const_spec((1, HEAD_DIM)),
            _const_spec((1, HEAD_DIM)),
            _const_spec((1, GMLP_WIDTH)),
            _const_spec((1, GMLP_WIDTH)),
            rope_spec, rope_spec, rope_spec,
            *cast_specs,
        ],
        out_specs=[
            pl.BlockSpec((tm, ATTN_WIDTH), row),
            pl.BlockSpec((tm, 2 * KV_WIDTH), row),
            pl.BlockSpec((tm, GMLP_WIDTH), row),
            pl.BlockSpec((tm, GMLP_WIDTH), row),
            *cast_specs,
        ],
        out_shape=[
            jax.ShapeDtypeStruct((t, ATTN_WIDTH), BF16),
            jax.ShapeDtypeStruct((t, 2 * KV_WIDTH), BF16),
            jax.ShapeDtypeStruct((t, GMLP_WIDTH), BF16),
            jax.ShapeDtypeStruct((t, GMLP_WIDTH), BF16),
            *[jax.ShapeDtypeStruct(w.shape, BF16) for w in w_cast],
        ],
        compiler_params=pltpu.CompilerParams(
            dimension_semantics=("arbitrary",),
            vmem_limit_bytes=V7X_VMEM_LIMIT_BYTES),
        name="in_proj",
    )(x, g_mix, w_in, g_q, g_k, g_ln, b_ln, *rope, *w_cast)


def _mixer_kernel(tiles_per_seq, n_tiles, x_ref, q_ref, kvp_ref, kvc_ref, kvn_ref, u_ref,
                  vg_ref, sink_ref, wsp_ref, bsp_ref, gao_ref, ggo_ref, wout_ref, gffn_ref,
                  o_ref, xg_ref, kv_buf, mix_buf, mixn_cur, mixn_prev):
    tm = x_ref.shape[0]
    nblk = tm // BLOCK
    step = pl.program_id(0)
    tile_in_seq = jnp.minimum(step, n_tiles - 1) % tiles_per_seq
    is_first = tile_in_seq == 0
    is_last = tile_in_seq == tiles_per_seq - 1

    @pl.when(step == 0)
    def _():
        mixn_cur[...] = jnp.zeros(mixn_cur.shape, BF16)

    mixn_prev[...] = mixn_cur[...]

    kv_buf[0:BLOCK, :] = kvp_ref[...]
    kv_buf[BLOCK:BLOCK + tm, :] = kvc_ref[...]
    kv_buf[BLOCK + tm:, :] = kvn_ref[...]

    qi = lax.broadcasted_iota(jnp.int32, (BLOCK, BLOCK), 0)
    kj = lax.broadcasted_iota(jnp.int32, (BLOCK, BLOCK), 1)
    zero = jnp.zeros((BLOCK, BLOCK), F32)
    prev_bias = jnp.where(kj >= qi, 0.0, NEG_INF).astype(F32)
    next_bias = jnp.where(kj <= qi, 0.0, NEG_INF).astype(F32)
    scale = 1.0 / math.sqrt(HEAD_DIM)

    def band_bias(b):
        pb, nb_ = prev_bias, next_bias
        if b == 0:
            pb = jnp.where(is_first, NEG_INF, pb)
        if b == nblk - 1:
            nb_ = jnp.where(is_last, NEG_INF, nb_)
        return jnp.concatenate([pb, zero, nb_], axis=1)[None]

    def scores(b, h0):
        r0 = b * BLOCK
        kvh = h0 // GQA_GROUP
        qs = jnp.concatenate(
            [q_ref[r0:r0 + BLOCK, h * HEAD_DIM:(h + 1) * HEAD_DIM]
             for h in range(h0, h0 + HEADS_PER_CHAIN)], axis=0)
        kwin = kv_buf[r0:r0 + 3 * BLOCK, kvh * HEAD_DIM:(kvh + 1) * HEAD_DIM]
        return lax.dot_general(qs, kwin, (((1,), (1,)), ((), ())),
                               preferred_element_type=F32)

    def softmax_pv(b, h0, s):
        r0 = b * BLOCK
        kvh = h0 // GQA_GROUP
        vwin = kv_buf[r0:r0 + 3 * BLOCK,
                      KV_WIDTH + kvh * HEAD_DIM:KV_WIDTH + (kvh + 1) * HEAD_DIM]
        s = s.reshape(HEADS_PER_CHAIN, BLOCK, 3 * BLOCK) + band_bias(b)
        sink2 = jnp.concatenate(
            [jnp.full((1, 1, 1), sink_ref[h] * LOG2_E, F32)
             for h in range(h0, h0 + HEADS_PER_CHAIN)], axis=0)
        m2 = jnp.maximum(jnp.max(s, axis=-1, keepdims=True) * (scale * LOG2_E), sink2)
        p = jnp.exp2(s * (scale * LOG2_E) - m2)
        denom = jnp.sum(p, axis=-1, keepdims=True) + jnp.exp2(sink2 - m2)
        o = jnp.dot(p.reshape(HEADS_PER_CHAIN * BLOCK, 3 * BLOCK).astype(BF16), vwin,
                    preferred_element_type=F32)
        o = o.reshape(HEADS_PER_CHAIN, BLOCK, HEAD_DIM) * (1.0 / denom)
        sq = jnp.zeros((BLOCK, HEAD_DIM), F32)
        for g in range(HEADS_PER_CHAIN):
            c0 = (h0 + g) * HEAD_DIM
            mix_buf[r0:r0 + BLOCK, c0:c0 + HEAD_DIM] = o[g]
            sq = sq + jnp.square(o[g])
        return sq

    def spatial_gate(hd):
        c0 = hd * HEAD_DIM
        vh = jnp.concatenate(
            [vg_ref[c * BLOCK:(c + 1) * BLOCK, c0:c0 + HEAD_DIM] for c in range(nblk)], axis=1)
        mixed = jnp.dot(wsp_ref[hd], vh, preferred_element_type=F32)
        bias_h = bsp_ref[hd]
        sq = []
        for c in range(nblk):
            uu = u_ref[c * BLOCK:(c + 1) * BLOCK, c0:c0 + HEAD_DIM].astype(F32)
            gated = uu * (mixed[:, c * BLOCK:(c + 1) * BLOCK] + bias_h)
            mix_buf[c * BLOCK:(c + 1) * BLOCK, ATTN_WIDTH + c0:ATTN_WIDTH + c0 + HEAD_DIM] = gated
            sq.append(jnp.square(gated))
        return jnp.concatenate(sq, axis=0)

    def group_norm_to(rows, col0, width, sumsq, g_ref):
        r = lax.rsqrt(jnp.sum(sumsq, axis=-1, keepdims=True) * (1.0 / width) + EPS)
        mixn_cur[rows, col0:col0 + width] = (
            mix_buf[rows, col0:col0 + width] * r * g_ref[...]).astype(BF16)

    chains = [(b, h0) for b in range(nblk) for h0 in range(0, N_ATTN_HEADS, HEADS_PER_CHAIN)]
    chains_per_slab = len(chains) * OUT_SLAB // D_MODEL
    sumsq_gm = jnp.zeros((tm, HEAD_DIM), F32)
    sumsq_attn = jnp.zeros((BLOCK, HEAD_DIM), F32)
    gates_done = 0
    s_next = scores(*chains[0])
    for k, (b, h0) in enumerate(chains):
        s_cur = s_next
        if k + 1 < len(chains):
            s_next = scores(*chains[k + 1])
        if k % chains_per_slab == 0:
            c0 = (k // chains_per_slab) * OUT_SLAB
            x1 = x_ref[:, c0:c0 + OUT_SLAB] + jnp.dot(
                mixn_prev[...], wout_ref[:, c0:c0 + OUT_SLAB], preferred_element_type=F32)
            o_ref[:, c0:c0 + OUT_SLAB] = x1
            xg_ref[:, c0:c0 + OUT_SLAB] = (x1 * gffn_ref[:, c0:c0 + OUT_SLAB]).astype(BF16)

        sq = softmax_pv(b, h0, s_cur)
        sumsq_attn = sq if h0 == 0 else sumsq_attn + sq
        if h0 + HEADS_PER_CHAIN == N_ATTN_HEADS:
            group_norm_to(slice(b * BLOCK, (b + 1) * BLOCK), 0, ATTN_WIDTH, sumsq_attn, gao_ref)

        if gates_done < N_GMLP_HEADS and k % chains_per_slab == chains_per_slab - 1:
            for hd in range(gates_done, gates_done + GATES_PER_SLAB):
                sumsq_gm = sumsq_gm + spatial_gate(hd)
            gates_done += GATES_PER_SLAB
            if gates_done == N_GMLP_HEADS:
                group_norm_to(slice(0, tm), ATTN_WIDTH, GMLP_WIDTH, sumsq_gm, ggo_ref)


def _mixer(x, q, kv, u, vg, sink, w_sp, b_sp, g_ao, g_go, w_out, g_ffn, seq):
    t = x.shape[0]
    tm = TM_MIX
    nblk = tm // BLOCK
    n_tiles = t // tm
    tiles_per_seq = seq // tm
    last_block = t // BLOCK - 1
    cur = lambda i: jnp.minimum(i, n_tiles - 1)
    row = lambda i: (cur(i), 0)
    lag = lambda i: (jnp.maximum(i - 1, 0), 0)
    return pl.pallas_call(
        functools.partial(_mixer_kernel, tiles_per_seq, n_tiles),
        grid=(n_tiles + 1,),
        in_specs=[
            pl.BlockSpec((tm, D_MODEL), lag),
            pl.BlockSpec((tm, ATTN_WIDTH), row),
            pl.BlockSpec((BLOCK, 2 * KV_WIDTH),
                         lambda i: (jnp.maximum(cur(i) * nblk - 1, 0), 0)),
            pl.BlockSpec((tm, 2 * KV_WIDTH), row),
            pl.BlockSpec((BLOCK, 2 * KV_WIDTH),
                         lambda i: (jnp.minimum((cur(i) + 1) * nblk, last_block), 0)),
            pl.BlockSpec((tm, GMLP_WIDTH), row),
            pl.BlockSpec((tm, GMLP_WIDTH), row),
            pl.BlockSpec(memory_space=pltpu.SMEM),
            _const_spec((N_GMLP_HEADS, BLOCK, BLOCK)),
            _const_spec((N_GMLP_HEADS, BLOCK, BLOCK)),
            _const_spec((1, ATTN_WIDTH)),
            _const_spec((1, GMLP_WIDTH)),
            _const_spec((D_MODEL, D_MODEL)),
            _const_spec((1, D_MODEL)),
        ],
        out_specs=[pl.BlockSpec((tm, D_MODEL), lag), pl.BlockSpec((tm, D_MODEL), lag)],
        out_shape=[jax.ShapeDtypeStruct((t, D_MODEL), F32),
                   jax.ShapeDtypeStruct((t, D_MODEL), BF16)],
        scratch_shapes=[
            pltpu.VMEM((tm + 2 * BLOCK, 2 * KV_WIDTH), BF16),
            pltpu.VMEM((tm, D_MODEL), F32),
            pltpu.VMEM((tm, D_MODEL), BF16),
            pltpu.VMEM((tm, D_MODEL), BF16),
        ],
        compiler_params=pltpu.CompilerParams(
            dimension_semantics=("arbitrary",),
            vmem_limit_bytes=V7X_VMEM_LIMIT_BYTES),
        name="mixer",
    )(x, q, kv, kv, kv, u, vg, sink, w_sp, b_sp, g_ao, g_go, w_out, g_ffn)


def _ffn_kernel(xg_ref, x_ref, wup_ref, wdn_ref, o_ref):
    f = pl.program_id(1)
    n_f = pl.num_programs(1)
    n_sub = wup_ref.shape[1] // TF_SUB

    def partial_sum():
        def act(j):
            up = jnp.dot(xg_ref[...], wup_ref[:, j * TF_SUB:(j + 1) * TF_SUB],
                         preferred_element_type=F32)
            return jnp.square(jnp.maximum(up, 0.0)).astype(BF16)

        acc = None
        a_next = act(0)
        for j in range(n_sub):
            a_cur = a_next
            if j + 1 < n_sub:
                a_next = act(j + 1)
            d = jnp.dot(a_cur, wdn_ref[j * TF_SUB:(j + 1) * TF_SUB, :],
                        preferred_element_type=F32)
            acc = d if acc is None else acc + d
        return acc

    @pl.when(f == 0)
    def _():
        o_ref[...] = partial_sum()

    @pl.when((f > 0) & (f < n_f - 1))
    def _():
        o_ref[...] += partial_sum()

    @pl.when(f == n_f - 1)
    def _():
        x = x_ref[...]
        r2 = 1.0 / (jnp.mean(x * x, axis=-1, keepdims=True) + EPS)
        o_ref[...] = x + r2 * (o_ref[...] + partial_sum())


def _ffn(xg, x, w_up, w_down):
    t = x.shape[0]
    tm, tf = TM_FFN, TF_FFN
    return pl.pallas_call(
        _ffn_kernel,
        grid=(t // tm, D_FF // tf),
        in_specs=[
            pl.BlockSpec((tm, D_MODEL), lambda i, f: (i, 0)),
            pl.BlockSpec((tm, D_MODEL), lambda i, f: (i, 0)),
            pl.BlockSpec((D_MODEL, tf), lambda i, f: (0, f)),
            pl.BlockSpec((tf, D_MODEL), lambda i, f: (f, 0)),
        ],
        out_specs=pl.BlockSpec((tm, D_MODEL), lambda i, f: (i, 0)),
        out_shape=jax.ShapeDtypeStruct((t, D_MODEL), F32),
        compiler_params=pltpu.CompilerParams(
            dimension_semantics=("arbitrary", "arbitrary"),
            vmem_limit_bytes=V7X_VMEM_LIMIT_BYTES),
        name="ffn",
    )(xg, x, w_up, w_down)


def _rope_tables(seq):
    pos = np.arange(seq, dtype=np.float64)
    inv_freq = ROPE_THETA ** (-np.arange(0, ROT_DIM, 2, dtype=np.float64) / ROT_DIM)
    ang = pos[:, None] * inv_freq[None, :]
    cos, sin = np.cos(ang), np.sin(ang)
    half = ROT_DIM // 2
    pad = HEAD_DIM - ROT_DIM
    cos_t = np.concatenate([cos, cos, np.ones((seq, pad))], axis=1)
    sina_t = np.concatenate([-sin, np.zeros((seq, HEAD_DIM - half))], axis=1)
    sinb_t = np.concatenate([np.zeros((seq, half)), sin, np.zeros((seq, pad))], axis=1)
    return tuple(jnp.asarray(t, dtype=F32) for t in (cos_t, sina_t, sinb_t))


def kernel(x_prompt, x_sample, g_mix, w_in, g_q, g_k, sink, g_v_ln, b_v_ln, w_spatial, b_spatial, g_attn_out, g_gmlp_out, w_out, g_ffn, w_up, w_down):
    streams = [x_prompt, x_sample]
    for l in range(g_mix.shape[0]):
        w_in_l = w_in[l].astype(BF16)
        b_sp_l = jnp.broadcast_to(b_spatial[l][:, :, None], (N_GMLP_HEADS, BLOCK, BLOCK))
        cast_jobs = [(w_up[l], w_out[l]),
                     (w_down[l], w_spatial[l].reshape(N_GMLP_HEADS * BLOCK, BLOCK))]
        proj_out, cast_out = [], []
        for x, w_cast in zip(streams, cast_jobs):
            seq = x.shape[1]
            *qkuv, w_a, w_b = _in_proj(
                x.reshape(-1, D_MODEL), g_mix[l][None], w_in_l, g_q[l][None], g_k[l][None],
                g_v_ln[l][None], b_v_ln[l][None], _rope_tables(seq), seq, w_cast)
            proj_out.append(qkuv)
            cast_out.append((w_a, w_b))
        (w_up_l, w_out_l), (w_down_l, w_sp_l) = cast_out
        w_sp_l = w_sp_l.reshape(N_GMLP_HEADS, BLOCK, BLOCK)
        mixed = [
            _mixer(x.reshape(-1, D_MODEL), *qkuv, sink[l], w_sp_l, b_sp_l, g_attn_out[l][None],
                   g_gmlp_out[l][None], w_out_l, g_ffn[l][None], x.shape[1])
            for x, qkuv in zip(streams, proj_out)]
        streams = [
            _ffn(xg, x1, w_up_l, w_down_l).reshape(x.shape)
            for x, (x1, xg) in zip(streams, mixed)]
    return tuple(streams)
```

```python
import functools
import math

import jax
import jax.numpy as jnp
import numpy as np
from jax import lax
from jax.experimental import pallas as pl
from jax.experimental.pallas import tpu as pltpu

D_MODEL = 2048
HEAD_DIM = 128
N_ATTN_HEADS = 8
N_KV_HEADS = 2
GQA_GROUP = N_ATTN_HEADS // N_KV_HEADS
ATTN_WIDTH = N_ATTN_HEADS * HEAD_DIM
KV_WIDTH = N_KV_HEADS * HEAD_DIM
N_GMLP_HEADS = 8
GMLP_WIDTH = N_GMLP_HEADS * HEAD_DIM
IN_WIDTH = ATTN_WIDTH + 2 * KV_WIDTH + 2 * GMLP_WIDTH
BLOCK = 128
ROPE_THETA = 500000.0
ROT_DIM = HEAD_DIM // 4
D_FF = 4 * D_MODEL
EPS = 1e-6
NEG_INF = -1e30
LOG2_E = math.log2(math.e)

Q_OFF = 0
KV_OFF = ATTN_WIDTH
U_OFF = KV_OFF + 2 * KV_WIDTH
VG_OFF = U_OFF + GMLP_WIDTH

TM_PROJ = 512
ROW_PARTS_PROJ = 4
TM_MIX = 512
HEADS_PER_CHAIN = 2
OUT_SLAB = 256
GATES_PER_SLAB = 2
TM_FFN = 512
TF_FFN = 2048
TF_SUB = 2048

V7X_VMEM_LIMIT_BYTES = 56 * 1024 * 1024

F32 = jnp.float32
BF16 = jnp.bfloat16


def _rms_scale(x):
    return lax.rsqrt(jnp.mean(x * x, axis=-1, keepdims=True) + EPS)


def _const_spec(shape):
    nd = len(shape)
    return pl.BlockSpec(shape, lambda *_: (0,) * nd, pipeline_mode=pl.Buffered(1))


def _in_proj_kernel(x_ref, gmix_ref, w_ref, gq_ref, gk_ref, gln_ref, bln_ref,
                    cos_ref, sina_ref, sinb_ref, wcast_a_ref, wcast_b_ref,
                    q_ref, kv_ref, u_ref, vg_ref, wcast_a_out_ref, wcast_b_out_ref):
    tm = x_ref.shape[0]
    rows_per_part = tm // ROW_PARTS_PROJ

    wcast_a_out_ref[...] = wcast_a_ref[...].astype(BF16)
    wcast_b_out_ref[...] = wcast_b_ref[...].astype(BF16)

    def normed(r0):
        x = x_ref[r0:r0 + rows_per_part, :]
        return (x * _rms_scale(x) * gmix_ref[...]).astype(BF16)

    def norm_rope(r0, xh, g):
        rows = slice(r0, r0 + rows_per_part)
        y = xh * _rms_scale(xh) * g
        return (y * cos_ref[rows, :]
                + pltpu.roll(y, HEAD_DIM - ROT_DIM // 2, 1) * sina_ref[rows, :]
                + pltpu.roll(y, ROT_DIM // 2, 1) * sinb_ref[rows, :])

    def vg_epilogue(r0, p):
        p = jax.nn.gelu(p)
        mu = jnp.mean(p, axis=-1, keepdims=True)
        pc = p - mu
        y = pc * lax.rsqrt(jnp.mean(pc * pc, axis=-1, keepdims=True) + EPS)
        vg_ref[r0:r0 + rows_per_part, :] = (y * gln_ref[...] + bln_ref[...]).astype(BF16)

    def u_epilogue(c0, r0, p):
        u_ref[r0:r0 + rows_per_part, c0:c0 + p.shape[1]] = jax.nn.gelu(p).astype(BF16)

    def q_epilogue(c0, r0, p):
        g = gq_ref[...]
        for lo in range(0, p.shape[1], HEAD_DIM):
            q_ref[r0:r0 + rows_per_part, c0 + lo:c0 + lo + HEAD_DIM] = norm_rope(
                r0, p[:, lo:lo + HEAD_DIM], g).astype(BF16)

    def k_epilogue(r0, p):
        g = gk_ref[...]
        for lo in range(0, KV_WIDTH, HEAD_DIM):
            kv_ref[r0:r0 + rows_per_part, lo:lo + HEAD_DIM] = norm_rope(
                r0, p[:, lo:lo + HEAD_DIM], g).astype(BF16)

    def v_epilogue(r0, p):
        kv_ref[r0:r0 + rows_per_part, KV_WIDTH:] = p.astype(BF16)

    groups = [
        (VG_OFF, GMLP_WIDTH, vg_epilogue),
        (Q_OFF, ATTN_WIDTH, functools.partial(q_epilogue, 0)),
        (KV_OFF, KV_WIDTH, k_epilogue),
        (U_OFF, GMLP_WIDTH, functools.partial(u_epilogue, 0)),
        (KV_OFF + KV_WIDTH, KV_WIDTH, v_epilogue),
    ]
    stages = [(part, g) for part in range(ROW_PARTS_PROJ) for g in range(len(groups))]
    h = {0: normed(0)}

    def proj(stage):
        part, g = stage
        col, width, _ = groups[g]
        return jnp.dot(h[part], w_ref[:, col:col + width], preferred_element_type=F32)

    p_next = proj(stages[0])
    for k, (part, g) in enumerate(stages):
        p_cur = p_next
        if g == 0 and part + 1 < ROW_PARTS_PROJ:
            h[part + 1] = normed((part + 1) * rows_per_part)
        if k + 1 < len(stages):
            p_next = proj(stages[k + 1])
        groups[g][2](part * rows_per_part, p_cur)


def _in_proj(x, g_mix, w_in, g_q, g_k, g_ln, b_ln, rope, seq, w_cast):
    t = x.shape[0]
    tm = TM_PROJ
    n_tiles = t // tm
    tiles_per_seq = seq // tm
    row = lambda i: (i, 0)
    cast_specs = [pl.BlockSpec((w.shape[0] // n_tiles, w.shape[1]), row) for w in w_cast]
    rope_spec = pl.BlockSpec((tm, HEAD_DIM), lambda i: (i % tiles_per_seq, 0))
    return pl.pallas_call(
        _in_proj_kernel,
        grid=(t // tm,),
        in_specs=[
            pl.BlockSpec((tm, D_MODEL), row),
            _const_spec((1, D_MODEL)),
            _const_spec((D_MODEL, IN_WIDTH)),
            _const_spec((1, HEAD_DIM)),
            _const_spec((1, HEAD_DIM)),
            _const_spec((1, GMLP_WIDTH)),
            _const_spec((1, GMLP_WIDTH)),
            rope_spec, rope_spec, rope_spec,
            *cast_specs,
        ],
        out_specs=[
            pl.BlockSpec((tm, ATTN_WIDTH), row),
            pl.BlockSpec((tm, 2 * KV_WIDTH), row),
            pl.BlockSpec((tm, GMLP_WIDTH), row),
            pl.BlockSpec((tm, GMLP_WIDTH), row),
            *cast_specs,
        ],
        out_shape=[
            jax.ShapeDtypeStruct((t, ATTN_WIDTH), BF16),
            jax.ShapeDtypeStruct((t, 2 * KV_WIDTH), BF16),
            jax.ShapeDtypeStruct((t, GMLP_WIDTH), BF16),
            jax.ShapeDtypeStruct((t, GMLP_WIDTH), BF16),
            *[jax.ShapeDtypeStruct(w.shape, BF16) for w in w_cast],
        ],
        compiler_params=pltpu.CompilerParams(
            dimension_semantics=("arbitrary",),
            vmem_limit_bytes=V7X_VMEM_LIMIT_BYTES),
        name="in_proj",
    )(x, g_mix, w_in, g_q, g_k, g_ln, b_ln, *rope, *w_cast)


def _mixer_kernel(tiles_per_seq, n_tiles, x_ref, q_ref, kvp_ref, kvc_ref, kvn_ref, u_ref,
                  vg_ref, sink_ref, wsp_ref, bsp_ref, gao_ref, ggo_ref, wout_ref, gffn_ref,
                  o_ref, xg_ref, kv_buf, mix_buf, mixn_cur, mixn_prev):
    tm = x_ref.shape[0]
    nblk = tm // BLOCK
    step = pl.program_id(0)
    tile_in_seq = jnp.minimum(step, n_tiles - 1) % tiles_per_seq
    is_first = tile_in_seq == 0
    is_last = tile_in_seq == tiles_per_seq - 1

    @pl.when(step == 0)
    def _():
        mixn_cur[...] = jnp.zeros(mixn_cur.shape, BF16)

    mixn_prev[...] = mixn_cur[...]

    kv_buf[0:BLOCK, :] = kvp_ref[...]
    kv_buf[BLOCK:BLOCK + tm, :] = kvc_ref[...]
    kv_buf[BLOCK + tm:, :] = kvn_ref[...]

    qi = lax.broadcasted_iota(jnp.int32, (BLOCK, BLOCK), 0)
    kj = lax.broadcasted_iota(jnp.int32, (BLOCK, BLOCK), 1)
    zero = jnp.zeros((BLOCK, BLOCK), F32)
    prev_bias = jnp.where(kj >= qi, 0.0, NEG_INF).astype(F32)
    next_bias = jnp.where(kj <= qi, 0.0, NEG_INF).astype(F32)
    scale = 1.0 / math.sqrt(HEAD_DIM)

    def band_bias(b):
        pb, nb_ = prev_bias, next_bias
        if b == 0:
            pb = jnp.where(is_first, NEG_INF, pb)
        if b == nblk - 1:
            nb_ = jnp.where(is_last, NEG_INF, nb_)
        return jnp.concatenate([pb, zero, nb_], axis=1)[None]

    def scores(b, h0):
        r0 = b * BLOCK
        kvh = h0 // GQA_GROUP
        qs = jnp.concatenate(
            [q_ref[r0:r0 + BLOCK, h * HEAD_DIM:(h + 1) * HEAD_DIM]
             for h in range(h0, h0 + HEADS_PER_CHAIN)], axis=0)
        kwin = kv_buf[r0:r0 + 3 * BLOCK, kvh * HEAD_DIM:(kvh + 1) * HEAD_DIM]
        return lax.dot_general(qs, kwin, (((1,), (1,)), ((), ())),
                               preferred_element_type=F32)

    def softmax_pv(b, h0, s):
        r0 = b * BLOCK
        kvh = h0 // GQA_GROUP
        vwin = kv_buf[r0:r0 + 3 * BLOCK,
                      KV_WIDTH + kvh * HEAD_DIM:KV_WIDTH + (kvh + 1) * HEAD_DIM]
        s = s.reshape(HEADS_PER_CHAIN, BLOCK, 3 * BLOCK) + band_bias(b)
        sink2 = jnp.concatenate(
            [jnp.full((1, 1, 1), sink_ref[h] * LOG2_E, F32)
             for h in range(h0, h0 + HEADS_PER_CHAIN)], axis=0)
        m2 = jnp.maximum(jnp.max(s, axis=-1, keepdims=True) * (scale * LOG2_E), sink2)
        p = jnp.exp2(s * (scale * LOG2_E) - m2)
        denom = jnp.sum(p, axis=-1, keepdims=True) + jnp.exp2(sink2 - m2)
        o = jnp.dot(p.reshape(HEADS_PER_CHAIN * BLOCK, 3 * BLOCK).astype(BF16), vwin,
                    preferred_element_type=F32)
        o = o.reshape(HEADS_PER_CHAIN, BLOCK, HEAD_DIM) * (1.0 / denom)
        sq = jnp.zeros((BLOCK, HEAD_DIM), F32)
        for g in range(HEADS_PER_CHAIN):
            c0 = (h0 + g) * HEAD_DIM
            mix_buf[r0:r0 + BLOCK, c0:c0 + HEAD_DIM] = o[g]
            sq = sq + jnp.square(o[g])
        return sq

    def spatial_gate(hd):
        c0 = hd * HEAD_DIM
        vh = jnp.concatenate(
            [vg_ref[c * BLOCK:(c + 1) * BLOCK, c0:c0 + HEAD_DIM] for c in range(nblk)], axis=1)
        mixed = jnp.dot(wsp_ref[hd], vh, preferred_element_type=F32)
        bias_h = bsp_ref[hd]
        sq = []
        for c in range(nblk):
            uu = u_ref[c * BLOCK:(c + 1) * BLOCK, c0:c0 + HEAD_DIM].astype(F32)
            gated = uu * (mixed[:, c * BLOCK:(c + 1) * BLOCK] + bias_h)
            mix_buf[c * BLOCK:(c + 1) * BLOCK, ATTN_WIDTH + c0:ATTN_WIDTH + c0 + HEAD_DIM] = gated
            sq.append(jnp.square(gated))
        return jnp.concatenate(sq, axis=0)

    def group_norm_to(rows, col0, width, sumsq, g_ref):
        r = lax.rsqrt(jnp.sum(sumsq, axis=-1, keepdims=True) * (1.0 / width) + EPS)
        mixn_cur[rows, col0:col0 + width] = (
            mix_buf[rows, col0:col0 + width] * r * g_ref[...]).astype(BF16)

    chains = [(b, h0) for b in range(nblk) for h0 in range(0, N_ATTN_HEADS, HEADS_PER_CHAIN)]
    chains_per_slab = len(chains) * OUT_SLAB // D_MODEL
    sumsq_gm = jnp.zeros((tm, HEAD_DIM), F32)
    sumsq_attn = jnp.zeros((BLOCK, HEAD_DIM), F32)
    gates_done = 0
    s_next = scores(*chains[0])
    for k, (b, h0) in enumerate(chains):
        s_cur = s_next
        if k + 1 < len(chains):
            s_next = scores(*chains[k + 1])
        if k % chains_per_slab == 0:
            c0 = (k // chains_per_slab) * OUT_SLAB
            x1 = x_ref[:, c0:c0 + OUT_SLAB] + jnp.dot(
                mixn_prev[...], wout_ref[:, c0:c0 + OUT_SLAB], preferred_element_type=F32)
            o_ref[:, c0:c0 + OUT_SLAB] = x1
            xg_ref[:, c0:c0 + OUT_SLAB] = (x1 * gffn_ref[:, c0:c0 + OUT_SLAB]).astype(BF16)

        sq = softmax_pv(b, h0, s_cur)
        sumsq_attn = sq if h0 == 0 else sumsq_attn + sq
        if h0 + HEADS_PER_CHAIN == N_ATTN_HEADS:
            group_norm_to(slice(b * BLOCK, (b + 1) * BLOCK), 0, ATTN_WIDTH, sumsq_attn, gao_ref)

        if gates_done < N_GMLP_HEADS and k % chains_per_slab == chains_per_slab - 1:
            for hd in range(gates_done, gates_done + GATES_PER_SLAB):
                sumsq_gm = sumsq_gm + spatial_gate(hd)
            gates_done += GATES_PER_SLAB
            if gates_done == N_GMLP_HEADS:
                group_norm_to(slice(0, tm), ATTN_WIDTH, GMLP_WIDTH, sumsq_gm, ggo_ref)


def _mixer(x, q, kv, u, vg, sink, w_sp, b_sp, g_ao, g_go, w_out, g_ffn, seq):
    t = x.shape[0]
    tm = TM_MIX
    nblk = tm // BLOCK
    n_tiles = t // tm
    tiles_per_seq = seq // tm
    last_block = t // BLOCK - 1
    cur = lambda i: jnp.minimum(i, n_tiles - 1)
    row = lambda i: (cur(i), 0)
    lag = lambda i: (jnp.maximum(i - 1, 0), 0)
    return pl.pallas_call(
        functools.partial(_mixer_kernel, tiles_per_seq, n_tiles),
        grid=(n_tiles + 1,),
        in_specs=[
            pl.BlockSpec((tm, D_MODEL), lag),
            pl.BlockSpec((tm, ATTN_WIDTH), row),
            pl.BlockSpec((BLOCK, 2 * KV_WIDTH),
                         lambda i: (jnp.maximum(cur(i) * nblk - 1, 0), 0)),
            pl.BlockSpec((tm, 2 * KV_WIDTH), row),
            pl.BlockSpec((BLOCK, 2 * KV_WIDTH),
                         lambda i: (jnp.minimum((cur(i) + 1) * nblk, last_block), 0)),
            pl.BlockSpec((tm, GMLP_WIDTH), row),
            pl.BlockSpec((tm, GMLP_WIDTH), row),
            pl.BlockSpec(memory_space=pltpu.SMEM),
            _const_spec((N_GMLP_HEADS, BLOCK, BLOCK)),
            _const_spec((N_GMLP_HEADS, BLOCK, BLOCK)),
            _const_spec((1, ATTN_WIDTH)),
            _const_spec((1, GMLP_WIDTH)),
            _const_spec((D_MODEL, D_MODEL)),
            _const_spec((1, D_MODEL)),
        ],
        out_specs=[pl.BlockSpec((tm, D_MODEL), lag), pl.BlockSpec((tm, D_MODEL), lag)],
        out_shape=[jax.ShapeDtypeStruct((t, D_MODEL), F32),
                   jax.ShapeDtypeStruct((t, D_MODEL), BF16)],
        scratch_shapes=[
            pltpu.VMEM((tm + 2 * BLOCK, 2 * KV_WIDTH), BF16),
            pltpu.VMEM((tm, D_MODEL), F32),
            pltpu.VMEM((tm, D_MODEL), BF16),
            pltpu.VMEM((tm, D_MODEL), BF16),
        ],
        compiler_params=pltpu.CompilerParams(
            dimension_semantics=("arbitrary",),
            vmem_limit_bytes=V7X_VMEM_LIMIT_BYTES),
        name="mixer",
    )(x, q, kv, kv, kv, u, vg, sink, w_sp, b_sp, g_ao, g_go, w_out, g_ffn)


def _ffn_kernel(xg_ref, x_ref, wup_ref, wdn_ref, o_ref):
    f = pl.program_id(1)
    n_f = pl.num_programs(1)
    n_sub = wup_ref.shape[1] // TF_SUB

    def partial_sum():
        def act(j):
            up = jnp.dot(xg_ref[...], wup_ref[:, j * TF_SUB:(j + 1) * TF_SUB],
                         preferred_element_type=F32)
            return jnp.square(jnp.maximum(up, 0.0)).astype(BF16)

        acc = None
        a_next = act(0)
        for j in range(n_sub):
            a_cur = a_next
            if j + 1 < n_sub:
                a_next = act(j + 1)
            d = jnp.dot(a_cur, wdn_ref[j * TF_SUB:(j + 1) * TF_SUB, :],
                        preferred_element_type=F32)
            acc = d if acc is None else acc + d
        return acc

    @pl.when(f == 0)
    def _():
        o_ref[...] = partial_sum()

    @pl.when((f > 0) & (f < n_f - 1))
    def _():
        o_ref[...] += partial_sum()

    @pl.when(f == n_f - 1)
    def _():
        x = x_ref[...]
        r2 = 1.0 / (jnp.mean(x * x, axis=-1, keepdims=True) + EPS)
        o_ref[...] = x + r2 * (o_ref[...] + partial_sum())


def _ffn(xg, x, w_up, w_down):
    t = x.shape[0]
    tm, tf = TM_FFN, TF_FFN
    return pl.pallas_call(
        _ffn_kernel,
        grid=(t // tm, D_FF // tf),
        in_specs=[
            pl.BlockSpec((tm, D_MODEL), lambda i, f: (i, 0)),
            pl.BlockSpec((tm, D_MODEL), lambda i, f: (i, 0)),
            pl.BlockSpec((D_MODEL, tf), lambda i, f: (0, f)),
            pl.BlockSpec((tf, D_MODEL), lambda i, f: (f, 0)),
        ],
        out_specs=pl.BlockSpec((tm, D_MODEL), lambda i, f: (i, 0)),
        out_shape=jax.ShapeDtypeStruct((t, D_MODEL), F32),
        compiler_params=pltpu.CompilerParams(
            dimension_semantics=("arbitrary", "arbitrary"),
            vmem_limit_bytes=V7X_VMEM_LIMIT_BYTES),
        name="ffn",
    )(xg, x, w_up, w_down)


def _rope_tables(seq):
    pos = np.arange(seq, dtype=np.float64)
    inv_freq = ROPE_THETA ** (-np.arange(0, ROT_DIM, 2, dtype=np.float64) / ROT_DIM)
    ang = pos[:, None] * inv_freq[None, :]
    cos, sin = np.cos(ang), np.sin(ang)
    half = ROT_DIM // 2
    pad = HEAD_DIM - ROT_DIM
    cos_t = np.concatenate([cos, cos, np.ones((seq, pad))], axis=1)
    sina_t = np.concatenate([-sin, np.zeros((seq, HEAD_DIM - half))], axis=1)
    sinb_t = np.concatenate([np.zeros((seq, half)), sin, np.zeros((seq, pad))], axis=1)
    return tuple(jnp.asarray(t, dtype=F32) for t in (cos_t, sina_t, sinb_t))


def kernel(x_prompt, x_sample, g_mix, w_in, g_q, g_k, sink, g_v_ln, b_v_ln, w_spatial, b_spatial, g_attn_out, g_gmlp_out, w_out, g_ffn, w_up, w_down):
    streams = [x_prompt, x_sample]
    for l in range(g_mix.shape[0]):
        w_in_l = w_in[l].astype(BF16)
        b_sp_l = jnp.broadcast_to(b_spatial[l][:, :, None], (N_GMLP_HEADS, BLOCK, BLOCK))
        cast_jobs = [(w_up[l], w_out[l]),
                     (w_down[l], w_spatial[l].reshape(N_GMLP_HEADS * BLOCK, BLOCK))]
        proj_out, cast_out = [], []
        for x, w_cast in zip(streams, cast_jobs):
            seq = x.shape[1]
            *qkuv, w_a, w_b = _in_proj(
                x.reshape(-1, D_MODEL), g_mix[l][None], w_in_l, g_q[l][None], g_k[l][None],
                g_v_ln[l][None], b_v_ln[l][None], _rope_tables(seq), seq, w_cast)
            proj_out.append(qkuv)
            cast_out.append((w_a, w_b))
        (w_up_l, w_out_l), (w_down_l, w_sp_l) = cast_out
        w_sp_l = w_sp_l.reshape(N_GMLP_HEADS, BLOCK, BLOCK)
        mixed = [
            _mixer(x.reshape(-1, D_MODEL), *qkuv, sink[l], w_sp_l, b_sp_l, g_attn_out[l][None],
                   g_gmlp_out[l][None], w_out_l, g_ffn[l][None], x.shape[1])
            for x, qkuv in zip(streams, proj_out)]
        streams = [
            _ffn(xg, x1, w_up_l, w_down_l).reshape(x.shape)
            for x, (x1, xg) in zip(streams, mixed)]
    return tuple(streams)
```

```python
import functools
import math

import jax
import jax.numpy as jnp
import numpy as np
from jax import lax
from jax.experimental import pallas as pl
from jax.experimental.pallas import tpu as pltpu

D_MODEL = 2048
HEAD_DIM = 128
N_ATTN_HEADS = 8
N_KV_HEADS = 2
GQA_GROUP = N_ATTN_HEADS // N_KV_HEADS
ATTN_WIDTH = N_ATTN_HEADS * HEAD_DIM
KV_WIDTH = N_KV_HEADS * HEAD_DIM
N_GMLP_HEADS = 8
GMLP_WIDTH = N_GMLP_HEADS * HEAD_DIM
IN_WIDTH = ATTN_WIDTH + 2 * KV_WIDTH + 2 * GMLP_WIDTH
BLOCK = 128
ROPE_THETA = 500000.0
ROT_DIM = HEAD_DIM // 4
D_FF = 4 * D_MODEL
EPS = 1e-6
NEG_INF = -1e30
LOG2_E = math.log2(math.e)

Q_OFF = 0
KV_OFF = ATTN_WIDTH
U_OFF = KV_OFF + 2 * KV_WIDTH
VG_OFF = U_OFF + GMLP_WIDTH

TM_PROJ = 512
ROW_PARTS_PROJ = 4
TM_MIX = 512
HEADS_PER_CHAIN = 2
OUT_SLAB = 256
GATES_PER_SLAB = 2
TM_FFN = 512
TF_FFN = 2048

V7X_VMEM_LIMIT_BYTES = 56 * 1024 * 1024

F32 = jnp.float32
BF16 = jnp.bfloat16


def _rms_scale(x):
    return lax.rsqrt(jnp.mean(x * x, axis=-1, keepdims=True) + EPS)


def _const_spec(shape):
    nd = len(shape)
    return pl.BlockSpec(shape, lambda *_: (0,) * nd, pipeline_mode=pl.Buffered(1))


def _in_proj_kernel(x_ref, gmix_ref, w_ref, gq_ref, gk_ref, gln_ref, bln_ref,
                    cos_ref, sina_ref, sinb_ref, wcast_a_ref, wcast_b_ref,
                    q_ref, kv_ref, u_ref, vg_ref, wcast_a_out_ref, wcast_b_out_ref):
    tm = x_ref.shape[0]
    rows_per_part = tm // ROW_PARTS_PROJ

    wcast_a_out_ref[...] = wcast_a_ref[...].astype(BF16)
    wcast_b_out_ref[...] = wcast_b_ref[...].astype(BF16)

    def normed(r0):
        x = x_ref[r0:r0 + rows_per_part, :]
        return (x * _rms_scale(x) * gmix_ref[...]).astype(BF16)

    def norm_rope(r0, xh, g):
        rows = slice(r0, r0 + rows_per_part)
        y = xh * _rms_scale(xh) * g
        return (y * cos_ref[rows, :]
                + pltpu.roll(y, HEAD_DIM - ROT_DIM // 2, 1) * sina_ref[rows, :]
                + pltpu.roll(y, ROT_DIM // 2, 1) * sinb_ref[rows, :])

    def vg_epilogue(r0, p):
        p = jax.nn.gelu(p)
        mu = jnp.mean(p, axis=-1, keepdims=True)
        pc = p - mu
        y = pc * lax.rsqrt(jnp.mean(pc * pc, axis=-1, keepdims=True) + EPS)
        vg_ref[r0:r0 + rows_per_part, :] = (y * gln_ref[...] + bln_ref[...]).astype(BF16)

    def u_epilogue(c0, r0, p):
        u_ref[r0:r0 + rows_per_part, c0:c0 + p.shape[1]] = jax.nn.gelu(p).astype(BF16)

    def q_epilogue(c0, r0, p):
        g = gq_ref[...]
        for lo in range(0, p.shape[1], HEAD_DIM):
            q_ref[r0:r0 + rows_per_part, c0 + lo:c0 + lo + HEAD_DIM] = norm_rope(
                r0, p[:, lo:lo + HEAD_DIM], g).astype(BF16)

    def k_epilogue(r0, p):
        g = gk_ref[...]
        for lo in range(0, KV_WIDTH, HEAD_DIM):
            kv_ref[r0:r0 + rows_per_part, lo:lo + HEAD_DIM] = norm_rope(
                r0, p[:, lo:lo + HEAD_DIM], g).astype(BF16)

    def v_epilogue(r0, p):
        kv_ref[r0:r0 + rows_per_part, KV_WIDTH:] = p.astype(BF16)

    groups = [
        (VG_OFF, GMLP_WIDTH, vg_epilogue),
        (Q_OFF, ATTN_WIDTH, functools.partial(q_epilogue, 0)),
        (KV_OFF, KV_WIDTH, k_epilogue),
        (U_OFF, GMLP_WIDTH, functools.partial(u_epilogue, 0)),
        (KV_OFF + KV_WIDTH, KV_WIDTH, v_epilogue),
    ]
    stages = [(part, g) for part in range(ROW_PARTS_PROJ) for g in range(len(groups))]
    h = {0: normed(0)}

    def proj(stage):
        part, g = stage
        col, width, _ = groups[g]
        return jnp.dot(h[part], w_ref[:, col:col + width], preferred_element_type=F32)

    p_next = proj(stages[0])
    for k, (part, g) in enumerate(stages):
        p_cur = p_next
        if g == 0 and part + 1 < ROW_PARTS_PROJ:
            h[part + 1] = normed((part + 1) * rows_per_part)
        if k + 1 < len(stages):
            p_next = proj(stages[k + 1])
        groups[g][2](part * rows_per_part, p_cur)


def _in_proj(x, g_mix, w_in, g_q, g_k, g_ln, b_ln, rope, seq, w_cast):
    t = x.shape[0]
    tm = TM_PROJ
    n_tiles = t // tm
    tiles_per_seq = seq // tm
    row = lambda i: (i, 0)
    cast_specs = [pl.BlockSpec((w.shape[0] // n_tiles, w.shape[1]), row) for w in w_cast]
    rope_spec = pl.BlockSpec((tm, HEAD_DIM), lambda i: (i % tiles_per_seq, 0))
    return pl.pallas_call(
        _in_proj_kernel,
        grid=(t // tm,),
        in_specs=[
            pl.BlockSpec((tm, D_MODEL), row),
            _const_spec((1, D_MODEL)),
            _const_spec((D_MODEL, IN_WIDTH)),
            _const_spec((1, HEAD_DIM)),
            _const_spec((1, HEAD_DIM)),
            _const_spec((1, GMLP_WIDTH)),
            _const_spec((1, GMLP_WIDTH)),
            rope_spec, rope_spec, rope_spec,
            *cast_specs,
        ],
        out_specs=[
            pl.BlockSpec((tm, ATTN_WIDTH), row),
            pl.BlockSpec((tm, 2 * KV_WIDTH), row),
            pl.BlockSpec((tm, GMLP_WIDTH), row),
            pl.BlockSpec((tm, GMLP_WIDTH), row),
            *cast_specs,
        ],
        out_shape=[
            jax.ShapeDtypeStruct((t, ATTN_WIDTH), BF16),
            jax.ShapeDtypeStruct((t, 2 * KV_WIDTH), BF16),
            jax.ShapeDtypeStruct((t, GMLP_WIDTH), BF16),
            jax.ShapeDtypeStruct((t, GMLP_WIDTH), BF16),
            *[jax.ShapeDtypeStruct(w.shape, BF16) for w in w_cast],
        ],
        compiler_params=pltpu.CompilerParams(
            dimension_semantics=("arbitrary",),
            vmem_limit_bytes=V7X_VMEM_LIMIT_BYTES),
        name="in_proj",
    )(x, g_mix, w_in, g_q, g_k, g_ln, b_ln, *rope, *w_cast)


def _mixer_kernel(tiles_per_seq, n_tiles, x_ref, q_ref, kvp_ref, kvc_ref, kvn_ref, u_ref,
                  vg_ref, sink_ref, wsp_ref, bsp_ref, gao_ref, ggo_ref, wout_ref, gffn_ref,
                  o_ref, xg_ref, mix_buf, mixn_cur, mixn_prev):
    tm = x_ref.shape[0]
    nblk = tm // BLOCK
    step = pl.program_id(0)
    tile_in_seq = jnp.minimum(step, n_tiles - 1) % tiles_per_seq
    is_first = tile_in_seq == 0
    is_last = tile_in_seq == tiles_per_seq - 1

    @pl.when(step == 0)
    def _():
        mixn_cur[...] = jnp.zeros(mixn_cur.shape, BF16)

    mixn_prev[...] = mixn_cur[...]

    def window(b, c0):
        cols = slice(c0, c0 + HEAD_DIM)
        start, stop = (b - 1) * BLOCK, (b + 2) * BLOCK
        parts = []
        if start < 0:
            parts.append(kvp_ref[:, cols])
        parts.append(kvc_ref[max(start, 0):min(stop, tm), cols])
        if stop > tm:
            parts.append(kvn_ref[:, cols])
        return jnp.concatenate(parts, axis=0)

    qi = lax.broadcasted_iota(jnp.int32, (BLOCK, BLOCK), 0)
    kj = lax.broadcasted_iota(jnp.int32, (BLOCK, BLOCK), 1)
    zero = jnp.zeros((BLOCK, BLOCK), F32)
    prev_bias = jnp.where(kj >= qi, 0.0, NEG_INF).astype(F32)
    next_bias = jnp.where(kj <= qi, 0.0, NEG_INF).astype(F32)
    scale = 1.0 / math.sqrt(HEAD_DIM)

    def band_bias(b):
        pb, nb_ = prev_bias, next_bias
        if b == 0:
            pb = jnp.where(is_first, NEG_INF, pb)
        if b == nblk - 1:
            nb_ = jnp.where(is_last, NEG_INF, nb_)
        return jnp.concatenate([pb, zero, nb_], axis=1)[None]

    def scores(b, h0):
        r0 = b * BLOCK
        kvh = h0 // GQA_GROUP
        qs = jnp.concatenate(
            [q_ref[r0:r0 + BLOCK, h * HEAD_DIM:(h + 1) * HEAD_DIM]
             for h in range(h0, h0 + HEADS_PER_CHAIN)], axis=0)
        kwin = window(b, kvh * HEAD_DIM)
        return lax.dot_general(qs, kwin, (((1,), (1,)), ((), ())),
                               preferred_element_type=F32)

    def softmax_pv(b, h0, s):
        r0 = b * BLOCK
        kvh = h0 // GQA_GROUP
        vwin = window(b, KV_WIDTH + kvh * HEAD_DIM)
        s = s.reshape(HEADS_PER_CHAIN, BLOCK, 3 * BLOCK) + band_bias(b)
        sink2 = jnp.concatenate(
            [jnp.full((1, 1, 1), sink_ref[h] * LOG2_E, F32)
             for h in range(h0, h0 + HEADS_PER_CHAIN)], axis=0)
        m2 = jnp.maximum(jnp.max(s, axis=-1, keepdims=True) * (scale * LOG2_E), sink2)
        p = jnp.exp2(s * (scale * LOG2_E) - m2)
        denom = jnp.sum(p, axis=-1, keepdims=True) + jnp.exp2(sink2 - m2)
        o = jnp.dot(p.reshape(HEADS_PER_CHAIN * BLOCK, 3 * BLOCK).astype(BF16), vwin,
                    preferred_element_type=F32)
        o = o.reshape(HEADS_PER_CHAIN, BLOCK, HEAD_DIM) * (1.0 / denom)
        sq = jnp.zeros((BLOCK, HEAD_DIM), F32)
        for g in range(HEADS_PER_CHAIN):
            c0 = (h0 + g) * HEAD_DIM
            mix_buf[r0:r0 + BLOCK, c0:c0 + HEAD_DIM] = o[g]
            sq = sq + jnp.square(o[g])
        return sq

    def spatial_gate(hd):
        c0 = hd * HEAD_DIM
        vh = jnp.concatenate(
            [vg_ref[c * BLOCK:(c + 1) * BLOCK, c0:c0 + HEAD_DIM] for c in range(nblk)], axis=1)
        mixed = jnp.dot(wsp_ref[hd], vh, preferred_element_type=F32)
        bias_h = bsp_ref[hd]
        sq = []
        for c in range(nblk):
            uu = u_ref[c * BLOCK:(c + 1) * BLOCK, c0:c0 + HEAD_DIM].astype(F32)
            gated = uu * (mixed[:, c * BLOCK:(c + 1) * BLOCK] + bias_h)
            mix_buf[c * BLOCK:(c + 1) * BLOCK, ATTN_WIDTH + c0:ATTN_WIDTH + c0 + HEAD_DIM] = gated
            sq.append(jnp.square(gated))
        return jnp.concatenate(sq, axis=0)

    def group_norm_to(rows, col0, width, sumsq, g_ref):
        r = lax.rsqrt(jnp.sum(sumsq, axis=-1, keepdims=True) * (1.0 / width) + EPS)
        mixn_cur[rows, col0:col0 + width] = (
            mix_buf[rows, col0:col0 + width] * r * g_ref[...]).astype(BF16)

    def project_slab(j):
        c0 = j * OUT_SLAB
        x1 = x_ref[:, c0:c0 + OUT_SLAB] + jnp.dot(
            mixn_prev[...], wout_ref[:, c0:c0 + OUT_SLAB], preferred_element_type=F32)
        o_ref[:, c0:c0 + OUT_SLAB] = x1
        xg_ref[:, c0:c0 + OUT_SLAB] = (x1 * gffn_ref[:, c0:c0 + OUT_SLAB]).astype(BF16)

    chains = [(b, h0) for b in range(nblk) for h0 in range(0, N_ATTN_HEADS, HEADS_PER_CHAIN)]
    chains_per_slab = len(chains) * OUT_SLAB // D_MODEL
    sumsq_gm = jnp.zeros((tm, HEAD_DIM), F32)
    sumsq_attn = jnp.zeros((BLOCK, HEAD_DIM), F32)
    gates_done = 0
    s_next = scores(*chains[0])
    for k, (b, h0) in enumerate(chains):
        s_cur = s_next
        if k + 1 < len(chains):
            s_next = scores(*chains[k + 1])
        if k % chains_per_slab == 0:
            project_slab(k // chains_per_slab)

        sq = softmax_pv(b, h0, s_cur)
        sumsq_attn = sq if h0 == 0 else sumsq_attn + sq
        if h0 + HEADS_PER_CHAIN == N_ATTN_HEADS:
            group_norm_to(slice(b * BLOCK, (b + 1) * BLOCK), 0, ATTN_WIDTH, sumsq_attn, gao_ref)

        if gates_done < N_GMLP_HEADS and k % chains_per_slab == chains_per_slab - 1:
            for hd in range(gates_done, gates_done + GATES_PER_SLAB):
                sumsq_gm = sumsq_gm + spatial_gate(hd)
            gates_done += GATES_PER_SLAB
            if gates_done == N_GMLP_HEADS:
                group_norm_to(slice(0, tm), ATTN_WIDTH, GMLP_WIDTH, sumsq_gm, ggo_ref)


def _mixer(x, q, kv, u, vg, sink, w_sp, b_sp, g_ao, g_go, w_out, g_ffn, seq):
    t = x.shape[0]
    tm = TM_MIX
    nblk = tm // BLOCK
    n_tiles = t // tm
    tiles_per_seq = seq // tm
    last_block = t // BLOCK - 1
    cur = lambda i: jnp.minimum(i, n_tiles - 1)
    row = lambda i: (cur(i), 0)
    lag = lambda i: (jnp.maximum(i - 1, 0), 0)
    return pl.pallas_call(
        functools.partial(_mixer_kernel, tiles_per_seq, n_tiles),
        grid=(n_tiles + 1,),
        in_specs=[
            pl.BlockSpec((tm, D_MODEL), lag),
            pl.BlockSpec((tm, ATTN_WIDTH), row),
            pl.BlockSpec((BLOCK, 2 * KV_WIDTH),
                         lambda i: (jnp.maximum(cur(i) * nblk - 1, 0), 0)),
            pl.BlockSpec((tm, 2 * KV_WIDTH), row),
            pl.BlockSpec((BLOCK, 2 * KV_WIDTH),
                         lambda i: (jnp.minimum((cur(i) + 1) * nblk, last_block), 0)),
            pl.BlockSpec((tm, GMLP_WIDTH), row),
            pl.BlockSpec((tm, GMLP_WIDTH), row),
            pl.BlockSpec(memory_space=pltpu.SMEM),
            _const_spec((N_GMLP_HEADS, BLOCK, BLOCK)),
            _const_spec((N_GMLP_HEADS, BLOCK, BLOCK)),
            _const_spec((1, ATTN_WIDTH)),
            _const_spec((1, GMLP_WIDTH)),
            _const_spec((D_MODEL, D_MODEL)),
            _const_spec((1, D_MODEL)),
        ],
        out_specs=[pl.BlockSpec((tm, D_MODEL), lag), pl.BlockSpec((tm, D_MODEL), lag)],
        out_shape=[jax.ShapeDtypeStruct((t, D_MODEL), F32),
                   jax.ShapeDtypeStruct((t, D_MODEL), BF16)],
        scratch_shapes=[
            pltpu.VMEM((tm, D_MODEL), F32),
            pltpu.VMEM((tm, D_MODEL), BF16),
            pltpu.VMEM((tm, D_MODEL), BF16),
        ],
        compiler_params=pltpu.CompilerParams(
            dimension_semantics=("arbitrary",),
            vmem_limit_bytes=V7X_VMEM_LIMIT_BYTES),
        name="mixer",
    )(x, q, kv, kv, kv, u, vg, sink, w_sp, b_sp, g_ao, g_go, w_out, g_ffn)


def _ffn_kernel(xg_ref, x_ref, wup_ref, wdn_ref, o_ref):
    f = pl.program_id(1)
    n_f = pl.num_programs(1)

    def partial_sum():
        up = jnp.dot(xg_ref[...], wup_ref[...], preferred_element_type=F32)
        act = jnp.square(jnp.maximum(up, 0.0)).astype(BF16)
        return jnp.dot(act, wdn_ref[...], preferred_element_type=F32)

    @pl.when(f == 0)
    def _():
        o_ref[...] = partial_sum()

    @pl.when((f > 0) & (f < n_f - 1))
    def _():
        o_ref[...] += partial_sum()

    @pl.when(f == n_f - 1)
    def _():
        x = x_ref[...]
        r2 = 1.0 / (jnp.mean(x * x, axis=-1, keepdims=True) + EPS)
        o_ref[...] = x + r2 * (o_ref[...] + partial_sum())


def _ffn(xg, x, w_up, w_down):
    t = x.shape[0]
    tm, tf = TM_FFN, TF_FFN
    return pl.pallas_call(
        _ffn_kernel,
        grid=(t // tm, D_FF // tf),
        in_specs=[
            pl.BlockSpec((tm, D_MODEL), lambda i, f: (i, 0)),
            pl.BlockSpec((tm, D_MODEL), lambda i, f: (i, 0)),
            pl.BlockSpec((D_MODEL, tf), lambda i, f: (0, f)),
            pl.BlockSpec((tf, D_MODEL), lambda i, f: (f, 0)),
        ],
        out_specs=pl.BlockSpec((tm, D_MODEL), lambda i, f: (i, 0)),
        out_shape=jax.ShapeDtypeStruct((t, D_MODEL), F32),
        compiler_params=pltpu.CompilerParams(
            dimension_semantics=("arbitrary", "arbitrary"),
            vmem_limit_bytes=V7X_VMEM_LIMIT_BYTES),
        name="ffn",
    )(xg, x, w_up, w_down)


def _rope_tables(seq):
    pos = np.arange(seq, dtype=np.float64)
    inv_freq = ROPE_THETA ** (-np.arange(0, ROT_DIM, 2, dtype=np.float64) / ROT_DIM)
    ang = pos[:, None] * inv_freq[None, :]
    cos, sin = np.cos(ang), np.sin(ang)
    half = ROT_DIM // 2
    pad = HEAD_DIM - ROT_DIM
    cos_t = np.concatenate([cos, cos, np.ones((seq, pad))], axis=1)
    sina_t = np.concatenate([-sin, np.zeros((seq, HEAD_DIM - half))], axis=1)
    sinb_t = np.concatenate([np.zeros((seq, half)), sin, np.zeros((seq, pad))], axis=1)
    return tuple(jnp.asarray(t, dtype=F32) for t in (cos_t, sina_t, sinb_t))


def kernel(x_prompt, x_sample, g_mix, w_in, g_q, g_k, sink, g_v_ln, b_v_ln, w_spatial, b_spatial, g_attn_out, g_gmlp_out, w_out, g_ffn, w_up, w_down):
    streams = [x_prompt, x_sample]
    for l in range(g_mix.shape[0]):
        w_in_l = w_in[l].astype(BF16)
        b_sp_l = jnp.broadcast_to(b_spatial[l][:, :, None], (N_GMLP_HEADS, BLOCK, BLOCK))
        cast_jobs = [(w_up[l], w_out[l]),
                     (w_down[l], w_spatial[l].reshape(N_GMLP_HEADS * BLOCK, BLOCK))]
        proj_out, cast_out = [], []
        for x, w_cast in zip(streams, cast_jobs):
            seq = x.shape[1]
            *qkuv, w_a, w_b = _in_proj(
                x.reshape(-1, D_MODEL), g_mix[l][None], w_in_l, g_q[l][None], g_k[l][None],
                g_v_ln[l][None], b_v_ln[l][None], _rope_tables(seq), seq, w_cast)
            proj_out.append(qkuv)
            cast_out.append((w_a, w_b))
        (w_up_l, w_out_l), (w_down_l, w_sp_l) = cast_out
        w_sp_l = w_sp_l.reshape(N_GMLP_HEADS, BLOCK, BLOCK)
        mixed = [
            _mixer(x.reshape(-1, D_MODEL), *qkuv, sink[l], w_sp_l, b_sp_l, g_attn_out[l][None],
                   g_gmlp_out[l][None], w_out_l, g_ffn[l][None], x.shape[1])
            for x, qkuv in zip(streams, proj_out)]
        streams = [
            _ffn(xg, x1, w_up_l, w_down_l).reshape(x.shape)
            for x, (x1, xg) in zip(streams, mixed)]
    return tuple(streams)
```

```python
import functools
import math

import jax
import jax.numpy as jnp
import numpy as np
from jax import lax
from jax.experimental import pallas as pl
from jax.experimental.pallas import tpu as pltpu

D_MODEL = 2048
HEAD_DIM = 128
N_ATTN_HEADS = 8
N_KV_HEADS = 2
GQA_GROUP = N_ATTN_HEADS // N_KV_HEADS
ATTN_WIDTH = N_ATTN_HEADS * HEAD_DIM
KV_WIDTH = N_KV_HEADS * HEAD_DIM
N_GMLP_HEADS = 8
GMLP_WIDTH = N_GMLP_HEADS * HEAD_DIM
IN_WIDTH = ATTN_WIDTH + 2 * KV_WIDTH + 2 * GMLP_WIDTH
BLOCK = 128
ROPE_THETA = 500000.0
ROT_DIM = HEAD_DIM // 4
D_FF = 4 * D_MODEL
EPS = 1e-6
NEG_INF = -1e30
LOG2_E = math.log2(math.e)

Q_OFF = 0
KV_OFF = ATTN_WIDTH
U_OFF = KV_OFF + 2 * KV_WIDTH
VG_OFF = U_OFF + GMLP_WIDTH

TM_PROJ = 512
ROW_PARTS_PROJ = 4
TM_MIX = 512
HEADS_PER_CHAIN = 2
OUT_SLAB = 256
SCORE_LOOKAHEAD = 3
GATES_PER_SLAB = 2
TM_FFN = 512
TF_FFN = 2048

V7X_VMEM_LIMIT_BYTES = 56 * 1024 * 1024

F32 = jnp.float32
BF16 = jnp.bfloat16


def _rms_scale(x):
    return lax.rsqrt(jnp.mean(x * x, axis=-1, keepdims=True) + EPS)


def _const_spec(shape):
    nd = len(shape)
    return pl.BlockSpec(shape, lambda *_: (0,) * nd, pipeline_mode=pl.Buffered(1))


def _in_proj_kernel(x_ref, gmix_ref, w_ref, gq_ref, gk_ref, gln_ref, bln_ref,
                    cos_ref, sina_ref, sinb_ref, wcast_a_ref, wcast_b_ref,
                    q_ref, kv_ref, u_ref, vg_ref, wcast_a_out_ref, wcast_b_out_ref):
    tm = x_ref.shape[0]
    rows_per_part = tm // ROW_PARTS_PROJ

    wcast_a_out_ref[...] = wcast_a_ref[...].astype(BF16)
    wcast_b_out_ref[...] = wcast_b_ref[...].astype(BF16)

    def normed(r0):
        x = x_ref[r0:r0 + rows_per_part, :]
        return (x * _rms_scale(x) * gmix_ref[...]).astype(BF16)

    def norm_rope(r0, xh, g):
        rows = slice(r0, r0 + rows_per_part)
        y = xh * _rms_scale(xh) * g
        return (y * cos_ref[rows, :]
                + pltpu.roll(y, HEAD_DIM - ROT_DIM // 2, 1) * sina_ref[rows, :]
                + pltpu.roll(y, ROT_DIM // 2, 1) * sinb_ref[rows, :])

    def vg_epilogue(r0, p):
        p = jax.nn.gelu(p)
        mu = jnp.mean(p, axis=-1, keepdims=True)
        pc = p - mu
        y = pc * lax.rsqrt(jnp.mean(pc * pc, axis=-1, keepdims=True) + EPS)
        vg_ref[r0:r0 + rows_per_part, :] = (y * gln_ref[...] + bln_ref[...]).astype(BF16)

    def u_epilogue(c0, r0, p):
        u_ref[r0:r0 + rows_per_part, c0:c0 + p.shape[1]] = jax.nn.gelu(p).astype(BF16)

    def q_epilogue(c0, r0, p):
        g = gq_ref[...]
        for lo in range(0, p.shape[1], HEAD_DIM):
            q_ref[r0:r0 + rows_per_part, c0 + lo:c0 + lo + HEAD_DIM] = norm_rope(
                r0, p[:, lo:lo + HEAD_DIM], g).astype(BF16)

    def k_epilogue(r0, p):
        g = gk_ref[...]
        for lo in range(0, KV_WIDTH, HEAD_DIM):
            kv_ref[r0:r0 + rows_per_part, lo:lo + HEAD_DIM] = norm_rope(
                r0, p[:, lo:lo + HEAD_DIM], g).astype(BF16)

    def v_epilogue(r0, p):
        kv_ref[r0:r0 + rows_per_part, KV_WIDTH:] = p.astype(BF16)

    groups = [
        (VG_OFF, GMLP_WIDTH, vg_epilogue),
        (Q_OFF, ATTN_WIDTH, functools.partial(q_epilogue, 0)),
        (KV_OFF, KV_WIDTH, k_epilogue),
        (U_OFF, GMLP_WIDTH, functools.partial(u_epilogue, 0)),
        (KV_OFF + KV_WIDTH, KV_WIDTH, v_epilogue),
    ]
    stages = [(part, g) for part in range(ROW_PARTS_PROJ) for g in range(len(groups))]
    h = {0: normed(0)}

    def proj(stage):
        part, g = stage
        col, width, _ = groups[g]
        return jnp.dot(h[part], w_ref[:, col:col + width], preferred_element_type=F32)

    p_next = proj(stages[0])
    for k, (part, g) in enumerate(stages):
        p_cur = p_next
        if g == 0 and part + 1 < ROW_PARTS_PROJ:
            h[part + 1] = normed((part + 1) * rows_per_part)
        if k + 1 < len(stages):
            p_next = proj(stages[k + 1])
        groups[g][2](part * rows_per_part, p_cur)


def _in_proj(x, g_mix, w_in, g_q, g_k, g_ln, b_ln, rope, seq, w_cast):
    t = x.shape[0]
    tm = TM_PROJ
    n_tiles = t // tm
    tiles_per_seq = seq // tm
    row = lambda i: (i, 0)
    cast_specs = [pl.BlockSpec((w.shape[0] // n_tiles, w.shape[1]), row) for w in w_cast]
    rope_spec = pl.BlockSpec((tm, HEAD_DIM), lambda i: (i % tiles_per_seq, 0))
    return pl.pallas_call(
        _in_proj_kernel,
        grid=(t // tm,),
        in_specs=[
            pl.BlockSpec((tm, D_MODEL), row),
            _const_spec((1, D_MODEL)),
            _const_spec((D_MODEL, IN_WIDTH)),
            _const_spec((1, HEAD_DIM)),
            _const_spec((1, HEAD_DIM)),
            _const_spec((1, GMLP_WIDTH)),
            _const_spec((1, GMLP_WIDTH)),
            rope_spec, rope_spec, rope_spec,
            *cast_specs,
        ],
        out_specs=[
            pl.BlockSpec((tm, ATTN_WIDTH), row),
            pl.BlockSpec((tm, 2 * KV_WIDTH), row),
            pl.BlockSpec((tm, GMLP_WIDTH), row),
            pl.BlockSpec((tm, GMLP_WIDTH), row),
            *cast_specs,
        ],
        out_shape=[
            jax.ShapeDtypeStruct((t, ATTN_WIDTH), BF16),
            jax.ShapeDtypeStruct((t, 2 * KV_WIDTH), BF16),
            jax.ShapeDtypeStruct((t, GMLP_WIDTH), BF16),
            jax.ShapeDtypeStruct((t, GMLP_WIDTH), BF16),
            *[jax.ShapeDtypeStruct(w.shape, BF16) for w in w_cast],
        ],
        compiler_params=pltpu.CompilerParams(
            dimension_semantics=("arbitrary",),
            vmem_limit_bytes=V7X_VMEM_LIMIT_BYTES),
        name="in_proj",
    )(x, g_mix, w_in, g_q, g_k, g_ln, b_ln, *rope, *w_cast)


def _project_slab(j, x_ref, mixn_ref, wout_ref, gffn_ref, o_ref, xg_ref):
    c0 = j * OUT_SLAB
    x1 = x_ref[:, c0:c0 + OUT_SLAB] + jnp.dot(
        mixn_ref[...], wout_ref[:, c0:c0 + OUT_SLAB], preferred_element_type=F32)
    o_ref[:, c0:c0 + OUT_SLAB] = x1
    xg_ref[:, c0:c0 + OUT_SLAB] = (x1 * gffn_ref[:, c0:c0 + OUT_SLAB]).astype(BF16)


def _mixer_kernel(tiles_per_seq, n_tiles, *refs):
    step = pl.program_id(0)
    pl.when(step < n_tiles)(functools.partial(_mixer_step, tiles_per_seq, *refs))
    pl.when(step == n_tiles)(functools.partial(_mixer_drain, *refs))


def _mixer_drain(x_ref, q_ref, kvp_ref, kvc_ref, kvn_ref, u_ref, vg_ref, sink_ref, wsp_ref,
                 bsp_ref, gao_ref, ggo_ref, wout_ref, gffn_ref, o_ref, xg_ref, mix_buf,
                 mixn_cur, mixn_prev):
    for j in range(D_MODEL // OUT_SLAB):
        _project_slab(j, x_ref, mixn_cur, wout_ref, gffn_ref, o_ref, xg_ref)


def _mixer_step(tiles_per_seq, x_ref, q_ref, kvp_ref, kvc_ref, kvn_ref, u_ref,
                vg_ref, sink_ref, wsp_ref, bsp_ref, gao_ref, ggo_ref, wout_ref, gffn_ref,
                o_ref, xg_ref, mix_buf, mixn_cur, mixn_prev):
    tm = x_ref.shape[0]
    nblk = tm // BLOCK
    step = pl.program_id(0)
    tile_in_seq = step % tiles_per_seq
    is_first = tile_in_seq == 0
    is_last = tile_in_seq == tiles_per_seq - 1

    @pl.when(step == 0)
    def _():
        mixn_cur[...] = jnp.zeros(mixn_cur.shape, BF16)

    mixn_prev[...] = mixn_cur[...]

    def window(b, c0):
        cols = slice(c0, c0 + HEAD_DIM)
        start, stop = (b - 1) * BLOCK, (b + 2) * BLOCK
        parts = []
        if start < 0:
            parts.append(kvp_ref[:, cols])
        parts.append(kvc_ref[max(start, 0):min(stop, tm), cols])
        if stop > tm:
            parts.append(kvn_ref[:, cols])
        return jnp.concatenate(parts, axis=0)

    qi = lax.broadcasted_iota(jnp.int32, (BLOCK, BLOCK), 0)
    kj = lax.broadcasted_iota(jnp.int32, (BLOCK, BLOCK), 1)
    zero = jnp.zeros((BLOCK, BLOCK), F32)
    prev_bias = jnp.where(kj >= qi, 0.0, NEG_INF).astype(F32)
    next_bias = jnp.where(kj <= qi, 0.0, NEG_INF).astype(F32)
    scale = 1.0 / math.sqrt(HEAD_DIM)

    def band_bias(b):
        pb, nb_ = prev_bias, next_bias
        if b == 0:
            pb = jnp.where(is_first, NEG_INF, pb)
        if b == nblk - 1:
            nb_ = jnp.where(is_last, NEG_INF, nb_)
        return jnp.concatenate([pb, zero, nb_], axis=1)[None]

    def scores(b, h0):
        r0 = b * BLOCK
        kvh = h0 // GQA_GROUP
        qs = jnp.concatenate(
            [q_ref[r0:r0 + BLOCK, h * HEAD_DIM:(h + 1) * HEAD_DIM]
             for h in range(h0, h0 + HEADS_PER_CHAIN)], axis=0)
        kwin = window(b, kvh * HEAD_DIM)
        return lax.dot_general(qs, kwin, (((1,), (1,)), ((), ())),
                               preferred_element_type=F32)

    def softmax_pv(b, h0, s):
        r0 = b * BLOCK
        kvh = h0 // GQA_GROUP
        vwin = window(b, KV_WIDTH + kvh * HEAD_DIM)
        s = s.reshape(HEADS_PER_CHAIN, BLOCK, 3 * BLOCK) + band_bias(b)
        sink2 = jnp.concatenate(
            [jnp.full((1, 1, 1), sink_ref[h] * LOG2_E, F32)
             for h in range(h0, h0 + HEADS_PER_CHAIN)], axis=0)
        m2 = jnp.maximum(jnp.max(s, axis=-1, keepdims=True) * (scale * LOG2_E), sink2)
        p = jnp.exp2(s * (scale * LOG2_E) - m2)
        denom = jnp.sum(p, axis=-1, keepdims=True) + jnp.exp2(sink2 - m2)
        o = jnp.dot(p.reshape(HEADS_PER_CHAIN * BLOCK, 3 * BLOCK).astype(BF16), vwin,
                    preferred_element_type=F32)
        o = o.reshape(HEADS_PER_CHAIN, BLOCK, HEAD_DIM) * (1.0 / denom)
        sq = jnp.zeros((BLOCK, HEAD_DIM), F32)
        for g in range(HEADS_PER_CHAIN):
            c0 = (h0 + g) * HEAD_DIM
            mix_buf[r0:r0 + BLOCK, c0:c0 + HEAD_DIM] = o[g]
            sq = sq + jnp.square(o[g])
        return sq

    def spatial_gate(hd):
        c0 = hd * HEAD_DIM
        vh = jnp.concatenate(
            [vg_ref[c * BLOCK:(c + 1) * BLOCK, c0:c0 + HEAD_DIM] for c in range(nblk)], axis=1)
        mixed = jnp.dot(wsp_ref[hd], vh, preferred_element_type=F32)
        bias_h = bsp_ref[hd]
        sq = []
        for c in range(nblk):
            uu = u_ref[c * BLOCK:(c + 1) * BLOCK, c0:c0 + HEAD_DIM].astype(F32)
            gated = uu * (mixed[:, c * BLOCK:(c + 1) * BLOCK] + bias_h)
            mix_buf[c * BLOCK:(c + 1) * BLOCK, ATTN_WIDTH + c0:ATTN_WIDTH + c0 + HEAD_DIM] = gated
            sq.append(jnp.square(gated))
        return jnp.concatenate(sq, axis=0)

    def group_norm_to(rows, col0, width, sumsq, g_ref):
        r = lax.rsqrt(jnp.sum(sumsq, axis=-1, keepdims=True) * (1.0 / width) + EPS)
        mixn_cur[rows, col0:col0 + width] = (
            mix_buf[rows, col0:col0 + width] * r * g_ref[...]).astype(BF16)

    chains = [(b, h0) for b in range(nblk) for h0 in range(0, N_ATTN_HEADS, HEADS_PER_CHAIN)]
    chains_per_slab = len(chains) * OUT_SLAB // D_MODEL
    sumsq_gm = jnp.zeros((tm, HEAD_DIM), F32)
    sumsq_attn = jnp.zeros((BLOCK, HEAD_DIM), F32)
    gates_done = 0
    pending = [scores(*chains[j]) for j in range(SCORE_LOOKAHEAD)]
    for k, (b, h0) in enumerate(chains):
        s_cur = pending.pop(0)
        if k + SCORE_LOOKAHEAD < len(chains):
            pending.append(scores(*chains[k + SCORE_LOOKAHEAD]))
        if k % chains_per_slab == 0:
            _project_slab(k // chains_per_slab, x_ref, mixn_prev, wout_ref, gffn_ref, o_ref,
                          xg_ref)

        sq = softmax_pv(b, h0, s_cur)
        sumsq_attn = sq if h0 == 0 else sumsq_attn + sq
        if h0 + HEADS_PER_CHAIN == N_ATTN_HEADS:
            group_norm_to(slice(b * BLOCK, (b + 1) * BLOCK), 0, ATTN_WIDTH, sumsq_attn, gao_ref)

        if gates_done < N_GMLP_HEADS and k % chains_per_slab == chains_per_slab - 1:
            for hd in range(gates_done, gates_done + GATES_PER_SLAB):
                sumsq_gm = sumsq_gm + spatial_gate(hd)
            gates_done += GATES_PER_SLAB
            if gates_done == N_GMLP_HEADS:
                group_norm_to(slice(0, tm), ATTN_WIDTH, GMLP_WIDTH, sumsq_gm, ggo_ref)


def _mixer(x, q, kv, u, vg, sink, w_sp, b_sp, g_ao, g_go, w_out, g_ffn, seq):
    t = x.shape[0]
    tm = TM_MIX
    nblk = tm // BLOCK
    n_tiles = t // tm
    tiles_per_seq = seq // tm
    last_block = t // BLOCK - 1
    cur = lambda i: jnp.minimum(i, n_tiles - 1)
    row = lambda i: (cur(i), 0)
    lag = lambda i: (jnp.maximum(i - 1, 0), 0)
    return pl.pallas_call(
        functools.partial(_mixer_kernel, tiles_per_seq, n_tiles),
        grid=(n_tiles + 1,),
        in_specs=[
            pl.BlockSpec((tm, D_MODEL), lag),
            pl.BlockSpec((tm, ATTN_WIDTH), row),
            pl.BlockSpec((BLOCK, 2 * KV_WIDTH),
                         lambda i: (jnp.maximum(cur(i) * nblk - 1, 0), 0)),
            pl.BlockSpec((tm, 2 * KV_WIDTH), row),
            pl.BlockSpec((BLOCK, 2 * KV_WIDTH),
                         lambda i: (jnp.minimum((cur(i) + 1) * nblk, last_block), 0)),
            pl.BlockSpec((tm, GMLP_WIDTH), row),
            pl.BlockSpec((tm, GMLP_WIDTH), row),
            pl.BlockSpec(memory_space=pltpu.SMEM),
            _const_spec((N_GMLP_HEADS, BLOCK, BLOCK)),
            _const_spec((N_GMLP_HEADS, BLOCK, BLOCK)),
            _const_spec((1, ATTN_WIDTH)),
            _const_spec((1, GMLP_WIDTH)),
            _const_spec((D_MODEL, D_MODEL)),
            _const_spec((1, D_MODEL)),
        ],
        out_specs=[pl.BlockSpec((tm, D_MODEL), lag), pl.BlockSpec((tm, D_MODEL), lag)],
        out_shape=[jax.ShapeDtypeStruct((t, D_MODEL), F32),
                   jax.ShapeDtypeStruct((t, D_MODEL), BF16)],
        scratch_shapes=[
            pltpu.VMEM((tm, D_MODEL), F32),
            pltpu.VMEM((tm, D_MODEL), BF16),
            pltpu.VMEM((tm, D_MODEL), BF16),
        ],
        compiler_params=pltpu.CompilerParams(
            dimension_semantics=("arbitrary",),
            vmem_limit_bytes=V7X_VMEM_LIMIT_BYTES),
        name="mixer",
    )(x, q, kv, kv, kv, u, vg, sink, w_sp, b_sp, g_ao, g_go, w_out, g_ffn)


def _ffn_kernel(xg_ref, x_ref, wup_ref, wdn_ref, o_ref):
    f = pl.program_id(1)
    n_f = pl.num_programs(1)

    def partial_sum():
        up = jnp.dot(xg_ref[...], wup_ref[...], preferred_element_type=F32)
        act = jnp.square(jnp.maximum(up, 0.0)).astype(BF16)
        return jnp.dot(act, wdn_ref[...], preferred_element_type=F32)

    @pl.when(f == 0)
    def _():
        o_ref[...] = partial_sum()

    @pl.when((f > 0) & (f < n_f - 1))
    def _():
        o_ref[...] += partial_sum()

    @pl.when(f == n_f - 1)
    def _():
        x = x_ref[...]
        r2 = 1.0 / (jnp.mean(x * x, axis=-1, keepdims=True) + EPS)
        o_ref[...] = x + r2 * (o_ref[...] + partial_sum())


def _ffn(xg, x, w_up, w_down):
    t = x.shape[0]
    tm, tf = TM_FFN, TF_FFN
    return pl.pallas_call(
        _ffn_kernel,
        grid=(t // tm, D_FF // tf),
        in_specs=[
            pl.BlockSpec((tm, D_MODEL), lambda i, f: (i, 0)),
            pl.BlockSpec((tm, D_MODEL), lambda i, f: (i, 0)),
            pl.BlockSpec((D_MODEL, tf), lambda i, f: (0, f)),
            pl.BlockSpec((tf, D_MODEL), lambda i, f: (f, 0)),
        ],
        out_specs=pl.BlockSpec((tm, D_MODEL), lambda i, f: (i, 0)),
        out_shape=jax.ShapeDtypeStruct((t, D_MODEL), F32),
        compiler_params=pltpu.CompilerParams(
            dimension_semantics=("arbitrary", "arbitrary"),
            vmem_limit_bytes=V7X_VMEM_LIMIT_BYTES),
        name="ffn",
    )(xg, x, w_up, w_down)


def _rope_tables(seq):
    pos = np.arange(seq, dtype=np.float64)
    inv_freq = ROPE_THETA ** (-np.arange(0, ROT_DIM, 2, dtype=np.float64) / ROT_DIM)
    ang = pos[:, None] * inv_freq[None, :]
    cos, sin = np.cos(ang), np.sin(ang)
    half = ROT_DIM // 2
    pad = HEAD_DIM - ROT_DIM
    cos_t = np.concatenate([cos, cos, np.ones((seq, pad))], axis=1)
    sina_t = np.concatenate([-sin, np.zeros((seq, HEAD_DIM - half))], axis=1)
    sinb_t = np.concatenate([np.zeros((seq, half)), sin, np.zeros((seq, pad))], axis=1)
    return tuple(jnp.asarray(t, dtype=F32) for t in (cos_t, sina_t, sinb_t))


def kernel(x_prompt, x_sample, g_mix, w_in, g_q, g_k, sink, g_v_ln, b_v_ln, w_spatial, b_spatial, g_attn_out, g_gmlp_out, w_out, g_ffn, w_up, w_down):
    streams = [x_prompt, x_sample]
    for l in range(g_mix.shape[0]):
        w_in_l = w_in[l].astype(BF16)
        b_sp_l = jnp.broadcast_to(b_spatial[l][:, :, None], (N_GMLP_HEADS, BLOCK, BLOCK))
        cast_jobs = [(w_up[l], w_out[l]),
                     (w_down[l], w_spatial[l].reshape(N_GMLP_HEADS * BLOCK, BLOCK))]
        proj_out, cast_out = [], []
        for x, w_cast in zip(streams, cast_jobs):
            seq = x.shape[1]
            *qkuv, w_a, w_b = _in_proj(
                x.reshape(-1, D_MODEL), g_mix[l][None], w_in_l, g_q[l][None], g_k[l][None],
                g_v_ln[l][None], b_v_ln[l][None], _rope_tables(seq), seq, w_cast)
            proj_out.append(qkuv)
            cast_out.append((w_a, w_b))
        (w_up_l, w_out_l), (w_down_l, w_sp_l) = cast_out
        w_sp_l = w_sp_l.reshape(N_GMLP_HEADS, BLOCK, BLOCK)
        mixed = [
            _mixer(x.reshape(-1, D_MODEL), *qkuv, sink[l], w_sp_l, b_sp_l, g_attn_out[l][None],
                   g_gmlp_out[l][None], w_out_l, g_ffn[l][None], x.shape[1])
            for x, qkuv in zip(streams, proj_out)]
        streams = [
            _ffn(xg, x1, w_up_l, w_down_l).reshape(x.shape)
            for x, (x1, xg) in zip(streams, mixed)]
    return tuple(streams)
```

```python
import functools
import math

import jax
import jax.numpy as jnp
import numpy as np
from jax import lax
from jax.experimental import pallas as pl
from jax.experimental.pallas import tpu as pltpu

D_MODEL = 2048
HEAD_DIM = 128
N_ATTN_HEADS = 8
N_KV_HEADS = 2
GQA_GROUP = N_ATTN_HEADS // N_KV_HEADS
ATTN_WIDTH = N_ATTN_HEADS * HEAD_DIM
KV_WIDTH = N_KV_HEADS * HEAD_DIM
N_GMLP_HEADS = 8
GMLP_WIDTH = N_GMLP_HEADS * HEAD_DIM
IN_WIDTH = ATTN_WIDTH + 2 * KV_WIDTH + 2 * GMLP_WIDTH
BLOCK = 128
ROPE_THETA = 500000.0
ROT_DIM = HEAD_DIM // 4
D_FF = 4 * D_MODEL
EPS = 1e-6
NEG_INF = -1e30
LOG2_E = math.log2(math.e)

Q_OFF = 0
KV_OFF = ATTN_WIDTH
U_OFF = KV_OFF + 2 * KV_WIDTH
VG_OFF = U_OFF + GMLP_WIDTH

TM_PROJ = 512
ROW_PARTS_PROJ = 4
TM_MIX = 512
HEADS_PER_CHAIN = 2
OUT_SLAB = 256
SCORE_LOOKAHEAD = 3
GATES_PER_SLAB = 2
TM_FFN = 512
TF_FFN = 2048

V7X_VMEM_LIMIT_BYTES = 56 * 1024 * 1024

F32 = jnp.float32
BF16 = jnp.bfloat16


def _rms_scale(x):
    return lax.rsqrt(jnp.mean(x * x, axis=-1, keepdims=True) + EPS)


def _const_spec(shape):
    nd = len(shape)
    return pl.BlockSpec(shape, lambda *_: (0,) * nd, pipeline_mode=pl.Buffered(1))


def _in_proj_kernel(x_ref, gmix_ref, w_ref, gq_ref, gk_ref, gln_ref, bln_ref,
                    cos_ref, sina_ref, sinb_ref, wcast_a_ref, wcast_b_ref,
                    q_ref, kv_ref, u_ref, vg_ref, wcast_a_out_ref, wcast_b_out_ref):
    tm = x_ref.shape[0]
    rows_per_part = tm // ROW_PARTS_PROJ

    wcast_a_out_ref[...] = wcast_a_ref[...].astype(BF16)
    wcast_b_out_ref[...] = wcast_b_ref[...].astype(BF16)

    def normed(r0):
        x = x_ref[r0:r0 + rows_per_part, :]
        return (x * _rms_scale(x) * gmix_ref[...]).astype(BF16)

    def norm_rope(r0, xh, g):
        rows = slice(r0, r0 + rows_per_part)
        y = xh * _rms_scale(xh) * g
        return (y * cos_ref[rows, :]
                + pltpu.roll(y, HEAD_DIM - ROT_DIM // 2, 1) * sina_ref[rows, :]
                + pltpu.roll(y, ROT_DIM // 2, 1) * sinb_ref[rows, :])

    def vg_epilogue(r0, p):
        p = jax.nn.gelu(p)
        mu = jnp.mean(p, axis=-1, keepdims=True)
        pc = p - mu
        y = pc * lax.rsqrt(jnp.mean(pc * pc, axis=-1, keepdims=True) + EPS)
        vg_ref[r0:r0 + rows_per_part, :] = (y * gln_ref[...] + bln_ref[...]).astype(BF16)

    def u_epilogue(c0, r0, p):
        u_ref[r0:r0 + rows_per_part, c0:c0 + p.shape[1]] = jax.nn.gelu(p).astype(BF16)

    def q_epilogue(c0, r0, p):
        g = gq_ref[...]
        for lo in range(0, p.shape[1], HEAD_DIM):
            q_ref[r0:r0 + rows_per_part, c0 + lo:c0 + lo + HEAD_DIM] = norm_rope(
                r0, p[:, lo:lo + HEAD_DIM], g).astype(BF16)

    def k_epilogue(r0, p):
        g = gk_ref[...]
        for lo in range(0, KV_WIDTH, HEAD_DIM):
            kv_ref[r0:r0 + rows_per_part, lo:lo + HEAD_DIM] = norm_rope(
                r0, p[:, lo:lo + HEAD_DIM], g).astype(BF16)

    def v_epilogue(r0, p):
        kv_ref[r0:r0 + rows_per_part, KV_WIDTH:] = p.astype(BF16)

    groups = [
        (VG_OFF, GMLP_WIDTH, vg_epilogue),
        (Q_OFF, ATTN_WIDTH, functools.partial(q_epilogue, 0)),
        (KV_OFF, KV_WIDTH, k_epilogue),
        (U_OFF, GMLP_WIDTH, functools.partial(u_epilogue, 0)),
        (KV_OFF + KV_WIDTH, KV_WIDTH, v_epilogue),
    ]
    stages = [(part, g) for part in range(ROW_PARTS_PROJ) for g in range(len(groups))]
    h = {0: normed(0)}

    def proj(stage):
        part, g = stage
        col, width, _ = groups[g]
        return jnp.dot(h[part], w_ref[:, col:col + width], preferred_element_type=F32)

    p_next = proj(stages[0])
    for k, (part, g) in enumerate(stages):
        p_cur = p_next
        if g == 0 and part + 1 < ROW_PARTS_PROJ:
            h[part + 1] = normed((part + 1) * rows_per_part)
        if k + 1 < len(stages):
            p_next = proj(stages[k + 1])
        groups[g][2](part * rows_per_part, p_cur)


def _in_proj(x, g_mix, w_in, g_q, g_k, g_ln, b_ln, rope, seq, w_cast):
    t = x.shape[0]
    tm = TM_PROJ
    n_tiles = t // tm
    tiles_per_seq = seq // tm
    row = lambda i: (i, 0)
    cast_specs = [pl.BlockSpec((w.shape[0] // n_tiles, w.shape[1]), row) for w in w_cast]
    rope_spec = pl.BlockSpec((tm, HEAD_DIM), lambda i: (i % tiles_per_seq, 0))
    return pl.pallas_call(
        _in_proj_kernel,
        grid=(t // tm,),
        in_specs=[
            pl.BlockSpec((tm, D_MODEL), row),
            _const_spec((1, D_MODEL)),
            _const_spec((D_MODEL, IN_WIDTH)),
            _const_spec((1, HEAD_DIM)),
            _const_spec((1, HEAD_DIM)),
            _const_spec((1, GMLP_WIDTH)),
            _const_spec((1, GMLP_WIDTH)),
            rope_spec, rope_spec, rope_spec,
            *cast_specs,
        ],
        out_specs=[
            pl.BlockSpec((tm, ATTN_WIDTH), row),
            pl.BlockSpec((tm, 2 * KV_WIDTH), row),
            pl.BlockSpec((tm, GMLP_WIDTH), row),
            pl.BlockSpec((tm, GMLP_WIDTH), row),
            *cast_specs,
        ],
        out_shape=[
            jax.ShapeDtypeStruct((t, ATTN_WIDTH), BF16),
            jax.ShapeDtypeStruct((t, 2 * KV_WIDTH), BF16),
            jax.ShapeDtypeStruct((t, GMLP_WIDTH), BF16),
            jax.ShapeDtypeStruct((t, GMLP_WIDTH), BF16),
            *[jax.ShapeDtypeStruct(w.shape, BF16) for w in w_cast],
        ],
        compiler_params=pltpu.CompilerParams(
            dimension_semantics=("arbitrary",),
            vmem_limit_bytes=V7X_VMEM_LIMIT_BYTES),
        name="in_proj",
    )(x, g_mix, w_in, g_q, g_k, g_ln, b_ln, *rope, *w_cast)


def _project_slab(j, x_ref, mixn_ref, wout_ref, gffn_ref, o_ref, xg_ref):
    c0 = j * OUT_SLAB
    x1 = x_ref[:, c0:c0 + OUT_SLAB] + jnp.dot(
        mixn_ref[...], wout_ref[:, c0:c0 + OUT_SLAB], preferred_element_type=F32)
    o_ref[:, c0:c0 + OUT_SLAB] = x1
    xg_ref[:, c0:c0 + OUT_SLAB] = (x1 * gffn_ref[:, c0:c0 + OUT_SLAB]).astype(BF16)


def _mixer_kernel(tiles_per_seq, n_tiles, *refs):
    step = pl.program_id(0)
    pl.when(step < n_tiles)(functools.partial(_mixer_step, tiles_per_seq, *refs))
    pl.when(step == n_tiles)(functools.partial(_mixer_drain, *refs))


def _mixer_drain(x_ref, q_ref, kvp_ref, kvc_ref, kvn_ref, u_ref, vg_ref, sink_ref, wsp_ref,
                 bsp_ref, gao_ref, ggo_ref, wout_ref, gffn_ref, o_ref, xg_ref, mix_buf,
                 mixn_cur, mixn_prev):
    for j in range(D_MODEL // OUT_SLAB):
        _project_slab(j, x_ref, mixn_cur, wout_ref, gffn_ref, o_ref, xg_ref)


def _mixer_step(tiles_per_seq, x_ref, q_ref, kvp_ref, kvc_ref, kvn_ref, u_ref,
                vg_ref, sink_ref, wsp_ref, bsp_ref, gao_ref, ggo_ref, wout_ref, gffn_ref,
                o_ref, xg_ref, mix_buf, mixn_cur, mixn_prev):
    tm = x_ref.shape[0]
    nblk = tm // BLOCK
    step = pl.program_id(0)
    tile_in_seq = step % tiles_per_seq
    is_first = tile_in_seq == 0
    is_last = tile_in_seq == tiles_per_seq - 1

    @pl.when(step == 0)
    def _():
        mixn_cur[...] = jnp.zeros(mixn_cur.shape, BF16)

    mixn_prev[...] = mixn_cur[...]

    def window(b, c0):
        cols = slice(c0, c0 + HEAD_DIM)
        start, stop = (b - 1) * BLOCK, (b + 2) * BLOCK
        parts = []
        if start < 0:
            parts.append(kvp_ref[:, cols])
        parts.append(kvc_ref[max(start, 0):min(stop, tm), cols])
        if stop > tm:
            parts.append(kvn_ref[:, cols])
        return jnp.concatenate(parts, axis=0)

    qi = lax.broadcasted_iota(jnp.int32, (BLOCK, BLOCK), 0)
    kj = lax.broadcasted_iota(jnp.int32, (BLOCK, BLOCK), 1)
    zero = jnp.zeros((BLOCK, BLOCK), F32)
    prev_bias = jnp.where(kj >= qi, 0.0, NEG_INF).astype(F32)
    next_bias = jnp.where(kj <= qi, 0.0, NEG_INF).astype(F32)
    scale = 1.0 / math.sqrt(HEAD_DIM)

    def band_bias(b):
        pb, nb_ = prev_bias, next_bias
        if b == 0:
            pb = jnp.where(is_first, NEG_INF, pb)
        if b == nblk - 1:
            nb_ = jnp.where(is_last, NEG_INF, nb_)
        return jnp.concatenate([pb, zero, nb_], axis=1)[None]

    def scores(b, h0):
        r0 = b * BLOCK
        kvh = h0 // GQA_GROUP
        qs = jnp.concatenate(
            [q_ref[r0:r0 + BLOCK, h * HEAD_DIM:(h + 1) * HEAD_DIM]
             for h in range(h0, h0 + HEADS_PER_CHAIN)], axis=0)
        kwin = window(b, kvh * HEAD_DIM)
        return lax.dot_general(qs, kwin, (((1,), (1,)), ((), ())),
                               preferred_element_type=F32)

    def softmax_pv(b, h0, s):
        r0 = b * BLOCK
        kvh = h0 // GQA_GROUP
        vwin = window(b, KV_WIDTH + kvh * HEAD_DIM)
        s = s.reshape(HEADS_PER_CHAIN, BLOCK, 3 * BLOCK) + band_bias(b)
        sink2 = jnp.concatenate(
            [jnp.full((1, 1, 1), sink_ref[h] * LOG2_E, F32)
             for h in range(h0, h0 + HEADS_PER_CHAIN)], axis=0)
        m2 = jnp.maximum(jnp.max(s, axis=-1, keepdims=True) * (scale * LOG2_E), sink2)
        p = jnp.exp2(s * (scale * LOG2_E) - m2)
        denom = jnp.sum(p, axis=-1, keepdims=True) + jnp.exp2(sink2 - m2)
        o = jnp.dot(p.reshape(HEADS_PER_CHAIN * BLOCK, 3 * BLOCK).astype(BF16), vwin,
                    preferred_element_type=F32)
        o = o.reshape(HEADS_PER_CHAIN, BLOCK, HEAD_DIM) * (1.0 / denom)
        sq = jnp.zeros((BLOCK, HEAD_DIM), F32)
        for g in range(HEADS_PER_CHAIN):
            c0 = (h0 + g) * HEAD_DIM
            mix_buf[r0:r0 + BLOCK, c0:c0 + HEAD_DIM] = o[g]
            sq = sq + jnp.square(o[g])
        return sq

    def spatial_gate(hd):
        c0 = hd * HEAD_DIM
        vh = jnp.concatenate(
            [vg_ref[c * BLOCK:(c + 1) * BLOCK, c0:c0 + HEAD_DIM] for c in range(nblk)], axis=1)
        mixed = jnp.dot(wsp_ref[hd], vh, preferred_element_type=F32)
        bias_h = bsp_ref[hd]
        sq = []
        for c in range(nblk):
            uu = u_ref[c * BLOCK:(c + 1) * BLOCK, c0:c0 + HEAD_DIM].astype(F32)
            gated = uu * (mixed[:, c * BLOCK:(c + 1) * BLOCK] + bias_h)
            mix_buf[c * BLOCK:(c + 1) * BLOCK, ATTN_WIDTH + c0:ATTN_WIDTH + c0 + HEAD_DIM] = gated
            sq.append(jnp.square(gated))
        return jnp.concatenate(sq, axis=0)

    def group_norm_to(rows, col0, width, sumsq, g_ref):
        r = lax.rsqrt(jnp.sum(sumsq, axis=-1, keepdims=True) * (1.0 / width) + EPS)
        mixn_cur[rows, col0:col0 + width] = (
            mix_buf[rows, col0:col0 + width] * r * g_ref[...]).astype(BF16)

    chains = [(b, h0) for b in range(nblk) for h0 in range(0, N_ATTN_HEADS, HEADS_PER_CHAIN)]
    chains_per_slab = len(chains) * OUT_SLAB // D_MODEL
    sumsq_gm = jnp.zeros((tm, HEAD_DIM), F32)
    sumsq_attn = jnp.zeros((BLOCK, HEAD_DIM), F32)
    gates_done = 0
    pending = [scores(*chains[j]) for j in range(SCORE_LOOKAHEAD)]
    for k, (b, h0) in enumerate(chains):
        s_cur = pending.pop(0)
        if k + SCORE_LOOKAHEAD < len(chains):
            pending.append(scores(*chains[k + SCORE_LOOKAHEAD]))
        if k % chains_per_slab == 0:
            _project_slab(k // chains_per_slab, x_ref, mixn_prev, wout_ref, gffn_ref, o_ref,
                          xg_ref)

        sq = softmax_pv(b, h0, s_cur)
        sumsq_attn = sq if h0 == 0 else sumsq_attn + sq
        if h0 + HEADS_PER_CHAIN == N_ATTN_HEADS:
            group_norm_to(slice(b * BLOCK, (b + 1) * BLOCK), 0, ATTN_WIDTH, sumsq_attn, gao_ref)

        if gates_done < N_GMLP_HEADS and k % chains_per_slab == chains_per_slab - 1:
            for hd in range(gates_done, gates_done + GATES_PER_SLAB):
                sumsq_gm = sumsq_gm + spatial_gate(hd)
            gates_done += GATES_PER_SLAB
            if gates_done == N_GMLP_HEADS:
                group_norm_to(slice(0, tm), ATTN_WIDTH, GMLP_WIDTH, sumsq_gm, ggo_ref)


def _mixer(x, q, kv, u, vg, sink, w_sp, b_sp, g_ao, g_go, w_out, g_ffn, seq):
    t = x.shape[0]
    tm = TM_MIX
    nblk = tm // BLOCK
    n_tiles = t // tm
    tiles_per_seq = seq // tm
    last_block = t // BLOCK - 1
    cur = lambda i: jnp.minimum(i, n_tiles - 1)
    row = lambda i: (cur(i), 0)
    lag = lambda i: (jnp.maximum(i - 1, 0), 0)
    return pl.pallas_call(
        functools.partial(_mixer_kernel, tiles_per_seq, n_tiles),
        grid=(n_tiles + 1,),
        in_specs=[
            pl.BlockSpec((tm, D_MODEL), lag),
            pl.BlockSpec((tm, ATTN_WIDTH), row),
            pl.BlockSpec((BLOCK, 2 * KV_WIDTH),
                         lambda i: (jnp.maximum(cur(i) * nblk - 1, 0), 0)),
            pl.BlockSpec((tm, 2 * KV_WIDTH), row),
            pl.BlockSpec((BLOCK, 2 * KV_WIDTH),
                         lambda i: (jnp.minimum((cur(i) + 1) * nblk, last_block), 0)),
            pl.BlockSpec((tm, GMLP_WIDTH), row),
            pl.BlockSpec((tm, GMLP_WIDTH), row),
            pl.BlockSpec(memory_space=pltpu.SMEM),
            _const_spec((N_GMLP_HEADS, BLOCK, BLOCK)),
            _const_spec((N_GMLP_HEADS, BLOCK, BLOCK)),
            _const_spec((1, ATTN_WIDTH)),
            _const_spec((1, GMLP_WIDTH)),
            _const_spec((D_MODEL, D_MODEL)),
            _const_spec((1, D_MODEL)),
        ],
        out_specs=[pl.BlockSpec((tm, D_MODEL), lag), pl.BlockSpec((tm, D_MODEL), lag)],
        out_shape=[jax.ShapeDtypeStruct((t, D_MODEL), F32),
                   jax.ShapeDtypeStruct((t, D_MODEL), BF16)],
        scratch_shapes=[
            pltpu.VMEM((tm, D_MODEL), F32),
            pltpu.VMEM((tm, D_MODEL), BF16),
            pltpu.VMEM((tm, D_MODEL), BF16),
        ],
        compiler_params=pltpu.CompilerParams(
            dimension_semantics=("arbitrary",),
            vmem_limit_bytes=V7X_VMEM_LIMIT_BYTES),
        name="mixer",
    )(x, q, kv, kv, kv, u, vg, sink, w_sp, b_sp, g_ao, g_go, w_out, g_ffn)


def _ffn_kernel(xg_ref, x_ref, wup_ref, wdn_ref, o_ref):
    f = pl.program_id(1)
    n_f = pl.num_programs(1)

    def partial_sum():
        up = jnp.dot(xg_ref[...], wup_ref[...], preferred_element_type=F32)
        act = jnp.square(jnp.maximum(up, 0.0)).astype(BF16)
        return jnp.dot(act, wdn_ref[...], preferred_element_type=F32)

    @pl.when(f == 0)
    def _():
        o_ref[...] = partial_sum()

    @pl.when((f > 0) & (f < n_f - 1))
    def _():
        o_ref[...] += partial_sum()

    @pl.when(f == n_f - 1)
    def _():
        x = x_ref[...]
        r2 = 1.0 / (jnp.mean(x * x, axis=-1, keepdims=True) + EPS)
        o_ref[...] = x + r2 * (o_ref[...] + partial_sum())


def _ffn(xg, x, w_up, w_down):
    t = x.shape[0]
    tm, tf = TM_FFN, TF_FFN
    stream = pltpu.emit_pipeline(
        _ffn_kernel,
        grid=(t // tm, D_FF // tf),
        in_specs=[
            pl.BlockSpec((tm, D_MODEL), lambda i, f: (i, 0)),
            pl.BlockSpec((tm, D_MODEL), lambda i, f: (i, 0)),
            pl.BlockSpec((D_MODEL, tf), lambda i, f: (0, f)),
            pl.BlockSpec((tf, D_MODEL), lambda i, f: (f, 0)),
        ],
        out_specs=[pl.BlockSpec((tm, D_MODEL), lambda i, f: (i, 0))],
        trace_scopes=False,
    )

    def outer(xg_hbm, x_hbm, wup_hbm, wdn_hbm, o_hbm):
        stream(xg_hbm, x_hbm, wup_hbm, wdn_hbm, o_hbm)

    any_spec = pl.BlockSpec(memory_space=pl.ANY)
    return pl.pallas_call(
        outer,
        in_specs=[any_spec] * 4,
        out_specs=any_spec,
        out_shape=jax.ShapeDtypeStruct((t, D_MODEL), F32),
        compiler_params=pltpu.CompilerParams(vmem_limit_bytes=V7X_VMEM_LIMIT_BYTES),
        name="ffn",
    )(xg, x, w_up, w_down)


def _rope_tables(seq):
    pos = np.arange(seq, dtype=np.float64)
    inv_freq = ROPE_THETA ** (-np.arange(0, ROT_DIM, 2, dtype=np.float64) / ROT_DIM)
    ang = pos[:, None] * inv_freq[None, :]
    cos, sin = np.cos(ang), np.sin(ang)
    half = ROT_DIM // 2
    pad = HEAD_DIM - ROT_DIM
    cos_t = np.concatenate([cos, cos, np.ones((seq, pad))], axis=1)
    sina_t = np.concatenate([-sin, np.zeros((seq, HEAD_DIM - half))], axis=1)
    sinb_t = np.concatenate([np.zeros((seq, half)), sin, np.zeros((seq, pad))], axis=1)
    return tuple(jnp.asarray(t, dtype=F32) for t in (cos_t, sina_t, sinb_t))


def kernel(x_prompt, x_sample, g_mix, w_in, g_q, g_k, sink, g_v_ln, b_v_ln, w_spatial, b_spatial, g_attn_out, g_gmlp_out, w_out, g_ffn, w_up, w_down):
    streams = [x_prompt, x_sample]
    for l in range(g_mix.shape[0]):
        w_in_l = w_in[l].astype(BF16)
        b_sp_l = jnp.broadcast_to(b_spatial[l][:, :, None], (N_GMLP_HEADS, BLOCK, BLOCK))
        cast_jobs = [(w_up[l], w_out[l]),
                     (w_down[l], w_spatial[l].reshape(N_GMLP_HEADS * BLOCK, BLOCK))]
        proj_out, cast_out = [], []
        for x, w_cast in zip(streams, cast_jobs):
            seq = x.shape[1]
            *qkuv, w_a, w_b = _in_proj(
                x.reshape(-1, D_MODEL), g_mix[l][None], w_in_l, g_q[l][None], g_k[l][None],
                g_v_ln[l][None], b_v_ln[l][None], _rope_tables(seq), seq, w_cast)
            proj_out.append(qkuv)
            cast_out.append((w_a, w_b))
        (w_up_l, w_out_l), (w_down_l, w_sp_l) = cast_out
        w_sp_l = w_sp_l.reshape(N_GMLP_HEADS, BLOCK, BLOCK)
        mixed = [
            _mixer(x.reshape(-1, D_MODEL), *qkuv, sink[l], w_sp_l, b_sp_l, g_attn_out[l][None],
                   g_gmlp_out[l][None], w_out_l, g_ffn[l][None], x.shape[1])
            for x, qkuv in zip(streams, proj_out)]
        streams = [
            _ffn(xg, x1, w_up_l, w_down_l).reshape(x.shape)
            for x, (x1, xg) in zip(streams, mixed)]
    return tuple(streams)
```

```python
import functools
import math

import jax
import jax.numpy as jnp
import numpy as np
from jax import lax
from jax.experimental import pallas as pl
from jax.experimental.pallas import tpu as pltpu

D_MODEL = 2048
HEAD_DIM = 128
N_ATTN_HEADS = 8
N_KV_HEADS = 2
GQA_GROUP = N_ATTN_HEADS // N_KV_HEADS
ATTN_WIDTH = N_ATTN_HEADS * HEAD_DIM
KV_WIDTH = N_KV_HEADS * HEAD_DIM
N_GMLP_HEADS = 8
GMLP_WIDTH = N_GMLP_HEADS * HEAD_DIM
IN_WIDTH = ATTN_WIDTH + 2 * KV_WIDTH + 2 * GMLP_WIDTH
BLOCK = 128
ROPE_THETA = 500000.0
ROT_DIM = HEAD_DIM // 4
D_FF = 4 * D_MODEL
EPS = 1e-6
NEG_INF = -1e30
LOG2_E = math.log2(math.e)

Q_OFF = 0
KV_OFF = ATTN_WIDTH
U_OFF = KV_OFF + 2 * KV_WIDTH
VG_OFF = U_OFF + GMLP_WIDTH

TM_PROJ = 512
ROW_PARTS_PROJ = 4
TM_MIX = 512
HEADS_PER_CHAIN = 2
OUT_SLAB = 256
SCORE_LOOKAHEAD = 3
GATES_PER_SLAB = 2
TM_FFN = 512
TF_FFN = 2048

V7X_VMEM_LIMIT_BYTES = 56 * 1024 * 1024

F32 = jnp.float32
BF16 = jnp.bfloat16


def _rms_scale(x):
    return lax.rsqrt(jnp.mean(x * x, axis=-1, keepdims=True) + EPS)


def _const_spec(shape):
    nd = len(shape)
    return pl.BlockSpec(shape, lambda *_: (0,) * nd, pipeline_mode=pl.Buffered(1))


def _in_proj_kernel(x_ref, gmix_ref, w_ref, gq_ref, gk_ref, gln_ref, bln_ref,
                    cos_ref, sina_ref, sinb_ref, wcast_a_ref, wcast_b_ref,
                    q_ref, kv_ref, u_ref, vg_ref, wcast_a_out_ref, wcast_b_out_ref):
    tm = x_ref.shape[0]
    rows_per_part = tm // ROW_PARTS_PROJ

    wcast_a_out_ref[...] = wcast_a_ref[...].astype(BF16)
    wcast_b_out_ref[...] = wcast_b_ref[...].astype(BF16)

    def normed(r0):
        x = x_ref[r0:r0 + rows_per_part, :]
        return (x * _rms_scale(x) * gmix_ref[...]).astype(BF16)

    def norm_rope(r0, xh, g):
        rows = slice(r0, r0 + rows_per_part)
        y = xh * _rms_scale(xh) * g
        return (y * cos_ref[rows, :]
                + pltpu.roll(y, HEAD_DIM - ROT_DIM // 2, 1) * sina_ref[rows, :]
                + pltpu.roll(y, ROT_DIM // 2, 1) * sinb_ref[rows, :])

    def vg_epilogue(r0, p):
        p = jax.nn.gelu(p)
        mu = jnp.mean(p, axis=-1, keepdims=True)
        pc = p - mu
        y = pc * lax.rsqrt(jnp.mean(pc * pc, axis=-1, keepdims=True) + EPS)
        vg_ref[r0:r0 + rows_per_part, :] = (y * gln_ref[...] + bln_ref[...]).astype(BF16)

    def u_epilogue(c0, r0, p):
        u_ref[r0:r0 + rows_per_part, c0:c0 + p.shape[1]] = jax.nn.gelu(p).astype(BF16)

    def q_epilogue(c0, r0, p):
        g = gq_ref[...]
        for lo in range(0, p.shape[1], HEAD_DIM):
            q_ref[r0:r0 + rows_per_part, c0 + lo:c0 + lo + HEAD_DIM] = norm_rope(
                r0, p[:, lo:lo + HEAD_DIM], g).astype(BF16)

    def k_epilogue(r0, p):
        g = gk_ref[...]
        for lo in range(0, KV_WIDTH, HEAD_DIM):
            kv_ref[r0:r0 + rows_per_part, lo:lo + HEAD_DIM] = norm_rope(
                r0, p[:, lo:lo + HEAD_DIM], g).astype(BF16)

    def v_epilogue(r0, p):
        kv_ref[r0:r0 + rows_per_part, KV_WIDTH:] = p.astype(BF16)

    groups = [
        (VG_OFF, GMLP_WIDTH, vg_epilogue),
        (Q_OFF, ATTN_WIDTH, functools.partial(q_epilogue, 0)),
        (KV_OFF, KV_WIDTH, k_epilogue),
        (U_OFF, GMLP_WIDTH, functools.partial(u_epilogue, 0)),
        (KV_OFF + KV_WIDTH, KV_WIDTH, v_epilogue),
    ]
    stages = [(part, g) for part in range(ROW_PARTS_PROJ) for g in range(len(groups))]
    h = {0: normed(0)}

    def proj(stage):
        part, g = stage
        col, width, _ = groups[g]
        return jnp.dot(h[part], w_ref[:, col:col + width], preferred_element_type=F32)

    p_next = proj(stages[0])
    for k, (part, g) in enumerate(stages):
        p_cur = p_next
        if g == 0 and part + 1 < ROW_PARTS_PROJ:
            h[part + 1] = normed((part + 1) * rows_per_part)
        if k + 1 < len(stages):
            p_next = proj(stages[k + 1])
        groups[g][2](part * rows_per_part, p_cur)


def _in_proj(x, g_mix, w_in, g_q, g_k, g_ln, b_ln, rope, seq, w_cast):
    t = x.shape[0]
    tm = TM_PROJ
    n_tiles = t // tm
    tiles_per_seq = seq // tm
    row = lambda i: (i, 0)
    cast_specs = [pl.BlockSpec((w.shape[0] // n_tiles, w.shape[1]), row) for w in w_cast]
    rope_spec = pl.BlockSpec((tm, HEAD_DIM), lambda i: (i % tiles_per_seq, 0))
    stream = pltpu.emit_pipeline(
        _in_proj_kernel,
        grid=(t // tm,),
        in_specs=[
            pl.BlockSpec((tm, D_MODEL), row),
            _const_spec((1, D_MODEL)),
            _const_spec((D_MODEL, IN_WIDTH)),
            _const_spec((1, HEAD_DIM)),
            _const_spec((1, HEAD_DIM)),
            _const_spec((1, GMLP_WIDTH)),
            _const_spec((1, GMLP_WIDTH)),
            rope_spec, rope_spec, rope_spec,
            *cast_specs,
        ],
        out_specs=[
            pl.BlockSpec((tm, ATTN_WIDTH), row),
            pl.BlockSpec((tm, 2 * KV_WIDTH), row),
            pl.BlockSpec((tm, GMLP_WIDTH), row),
            pl.BlockSpec((tm, GMLP_WIDTH), row),
            *cast_specs,
        ],
        trace_scopes=False,
    )

    def outer(*hbm_refs):
        stream(*hbm_refs)

    any_spec = pl.BlockSpec(memory_space=pl.ANY)
    return pl.pallas_call(
        outer,
        in_specs=[any_spec] * 12,
        out_specs=[any_spec] * 6,
        out_shape=[
            jax.ShapeDtypeStruct((t, ATTN_WIDTH), BF16),
            jax.ShapeDtypeStruct((t, 2 * KV_WIDTH), BF16),
            jax.ShapeDtypeStruct((t, GMLP_WIDTH), BF16),
            jax.ShapeDtypeStruct((t, GMLP_WIDTH), BF16),
            *[jax.ShapeDtypeStruct(w.shape, BF16) for w in w_cast],
        ],
        compiler_params=pltpu.CompilerParams(vmem_limit_bytes=V7X_VMEM_LIMIT_BYTES),
        name="in_proj",
    )(x, g_mix, w_in, g_q, g_k, g_ln, b_ln, *rope, *w_cast)


def _project_slab(j, x_ref, mixn_ref, wout_ref, gffn_ref, o_ref, xg_ref):
    c0 = j * OUT_SLAB
    x1 = x_ref[:, c0:c0 + OUT_SLAB] + jnp.dot(
        mixn_ref[...], wout_ref[:, c0:c0 + OUT_SLAB], preferred_element_type=F32)
    o_ref[:, c0:c0 + OUT_SLAB] = x1
    xg_ref[:, c0:c0 + OUT_SLAB] = (x1 * gffn_ref[:, c0:c0 + OUT_SLAB]).astype(BF16)


def _mixer_kernel(tiles_per_seq, n_tiles, *refs):
    step = pl.program_id(0)
    pl.when(step < n_tiles)(functools.partial(_mixer_step, tiles_per_seq, *refs))
    pl.when(step == n_tiles)(functools.partial(_mixer_drain, *refs))


def _mixer_drain(x_ref, q_ref, kvp_ref, kvc_ref, kvn_ref, u_ref, vg_ref, sink_ref, wsp_ref,
                 bsp_ref, gao_ref, ggo_ref, wout_ref, gffn_ref, o_ref, xg_ref, mix_buf,
                 mixn_cur, mixn_prev):
    for j in range(D_MODEL // OUT_SLAB):
        _project_slab(j, x_ref, mixn_cur, wout_ref, gffn_ref, o_ref, xg_ref)


def _mixer_step(tiles_per_seq, x_ref, q_ref, kvp_ref, kvc_ref, kvn_ref, u_ref,
                vg_ref, sink_ref, wsp_ref, bsp_ref, gao_ref, ggo_ref, wout_ref, gffn_ref,
                o_ref, xg_ref, mix_buf, mixn_cur, mixn_prev):
    tm = x_ref.shape[0]
    nblk = tm // BLOCK
    step = pl.program_id(0)
    tile_in_seq = step % tiles_per_seq
    is_first = tile_in_seq == 0
    is_last = tile_in_seq == tiles_per_seq - 1

    @pl.when(step == 0)
    def _():
        mixn_cur[...] = jnp.zeros(mixn_cur.shape, BF16)

    mixn_prev[...] = mixn_cur[...]

    def window(b, c0):
        cols = slice(c0, c0 + HEAD_DIM)
        start, stop = (b - 1) * BLOCK, (b + 2) * BLOCK
        parts = []
        if start < 0:
            parts.append(kvp_ref[:, cols])
        parts.append(kvc_ref[max(start, 0):min(stop, tm), cols])
        if stop > tm:
            parts.append(kvn_ref[:, cols])
        return jnp.concatenate(parts, axis=0)

    qi = lax.broadcasted_iota(jnp.int32, (BLOCK, BLOCK), 0)
    kj = lax.broadcasted_iota(jnp.int32, (BLOCK, BLOCK), 1)
    zero = jnp.zeros((BLOCK, BLOCK), F32)
    prev_bias = jnp.where(kj >= qi, 0.0, NEG_INF).astype(F32)
    next_bias = jnp.where(kj <= qi, 0.0, NEG_INF).astype(F32)
    scale = 1.0 / math.sqrt(HEAD_DIM)

    def band_bias(b):
        pb, nb_ = prev_bias, next_bias
        if b == 0:
            pb = jnp.where(is_first, NEG_INF, pb)
        if b == nblk - 1:
            nb_ = jnp.where(is_last, NEG_INF, nb_)
        return jnp.concatenate([pb, zero, nb_], axis=1)[None]

    def scores(b, h0):
        r0 = b * BLOCK
        kvh = h0 // GQA_GROUP
        qs = jnp.concatenate(
            [q_ref[r0:r0 + BLOCK, h * HEAD_DIM:(h + 1) * HEAD_DIM]
             for h in range(h0, h0 + HEADS_PER_CHAIN)], axis=0)
        kwin = window(b, kvh * HEAD_DIM)
        return lax.dot_general(qs, kwin, (((1,), (1,)), ((), ())),
                               preferred_element_type=F32)

    def softmax_pv(b, h0, s):
        r0 = b * BLOCK
        kvh = h0 // GQA_GROUP
        vwin = window(b, KV_WIDTH + kvh * HEAD_DIM)
        s = s.reshape(HEADS_PER_CHAIN, BLOCK, 3 * BLOCK) + band_bias(b)
        sink2 = jnp.concatenate(
            [jnp.full((1, 1, 1), sink_ref[h] * LOG2_E, F32)
             for h in range(h0, h0 + HEADS_PER_CHAIN)], axis=0)
        m2 = jnp.maximum(jnp.max(s, axis=-1, keepdims=True) * (scale * LOG2_E), sink2)
        p = jnp.exp2(s * (scale * LOG2_E) - m2)
        denom = jnp.sum(p, axis=-1, keepdims=True) + jnp.exp2(sink2 - m2)
        o = jnp.dot(p.reshape(HEADS_PER_CHAIN * BLOCK, 3 * BLOCK).astype(BF16), vwin,
                    preferred_element_type=F32)
        o = o.reshape(HEADS_PER_CHAIN, BLOCK, HEAD_DIM) * (1.0 / denom)
        sq = jnp.zeros((BLOCK, HEAD_DIM), F32)
        for g in range(HEADS_PER_CHAIN):
            c0 = (h0 + g) * HEAD_DIM
            mix_buf[r0:r0 + BLOCK, c0:c0 + HEAD_DIM] = o[g]
            sq = sq + jnp.square(o[g])
        return sq

    def spatial_gate(hd):
        c0 = hd * HEAD_DIM
        vh = jnp.concatenate(
            [vg_ref[c * BLOCK:(c + 1) * BLOCK, c0:c0 + HEAD_DIM] for c in range(nblk)], axis=1)
        mixed = jnp.dot(wsp_ref[hd], vh, preferred_element_type=F32)
        bias_h = bsp_ref[hd]
        sq = []
        for c in range(nblk):
            uu = u_ref[c * BLOCK:(c + 1) * BLOCK, c0:c0 + HEAD_DIM].astype(F32)
            gated = uu * (mixed[:, c * BLOCK:(c + 1) * BLOCK] + bias_h)
            mix_buf[c * BLOCK:(c + 1) * BLOCK, ATTN_WIDTH + c0:ATTN_WIDTH + c0 + HEAD_DIM] = gated
            sq.append(jnp.square(gated))
        return jnp.concatenate(sq, axis=0)

    def group_norm_to(rows, col0, width, sumsq, g_ref):
        r = lax.rsqrt(jnp.sum(sumsq, axis=-1, keepdims=True) * (1.0 / width) + EPS)
        mixn_cur[rows, col0:col0 + width] = (
            mix_buf[rows, col0:col0 + width] * r * g_ref[...]).astype(BF16)

    chains = [(b, h0) for b in range(nblk) for h0 in range(0, N_ATTN_HEADS, HEADS_PER_CHAIN)]
    chains_per_slab = len(chains) * OUT_SLAB // D_MODEL
    sumsq_gm = jnp.zeros((tm, HEAD_DIM), F32)
    sumsq_attn = jnp.zeros((BLOCK, HEAD_DIM), F32)
    gates_done = 0
    pending = [scores(*chains[j]) for j in range(SCORE_LOOKAHEAD)]
    for k, (b, h0) in enumerate(chains):
        s_cur = pending.pop(0)
        if k + SCORE_LOOKAHEAD < len(chains):
            pending.append(scores(*chains[k + SCORE_LOOKAHEAD]))
        if k % chains_per_slab == 0:
            _project_slab(k // chains_per_slab, x_ref, mixn_prev, wout_ref, gffn_ref, o_ref,
                          xg_ref)

        sq = softmax_pv(b, h0, s_cur)
        sumsq_attn = sq if h0 == 0 else sumsq_attn + sq
        if h0 + HEADS_PER_CHAIN == N_ATTN_HEADS:
            group_norm_to(slice(b * BLOCK, (b + 1) * BLOCK), 0, ATTN_WIDTH, sumsq_attn, gao_ref)

        if gates_done < N_GMLP_HEADS and k % chains_per_slab == chains_per_slab - 1:
            for hd in range(gates_done, gates_done + GATES_PER_SLAB):
                sumsq_gm = sumsq_gm + spatial_gate(hd)
            gates_done += GATES_PER_SLAB
            if gates_done == N_GMLP_HEADS:
                group_norm_to(slice(0, tm), ATTN_WIDTH, GMLP_WIDTH, sumsq_gm, ggo_ref)


def _mixer(x, q, kv, u, vg, sink, w_sp, b_sp, g_ao, g_go, w_out, g_ffn, seq):
    t = x.shape[0]
    tm = TM_MIX
    nblk = tm // BLOCK
    n_tiles = t // tm
    tiles_per_seq = seq // tm
    last_block = t // BLOCK - 1
    cur = lambda i: jnp.minimum(i, n_tiles - 1)
    row = lambda i: (cur(i), 0)
    lag = lambda i: (jnp.maximum(i - 1, 0), 0)
    return pl.pallas_call(
        functools.partial(_mixer_kernel, tiles_per_seq, n_tiles),
        grid=(n_tiles + 1,),
        in_specs=[
            pl.BlockSpec((tm, D_MODEL), lag),
            pl.BlockSpec((tm, ATTN_WIDTH), row),
            pl.BlockSpec((BLOCK, 2 * KV_WIDTH),
                         lambda i: (jnp.maximum(cur(i) * nblk - 1, 0), 0)),
            pl.BlockSpec((tm, 2 * KV_WIDTH), row),
            pl.BlockSpec((BLOCK, 2 * KV_WIDTH),
                         lambda i: (jnp.minimum((cur(i) + 1) * nblk, last_block), 0)),
            pl.BlockSpec((tm, GMLP_WIDTH), row),
            pl.BlockSpec((tm, GMLP_WIDTH), row),
            pl.BlockSpec(memory_space=pltpu.SMEM),
            _const_spec((N_GMLP_HEADS, BLOCK, BLOCK)),
            _const_spec((N_GMLP_HEADS, BLOCK, BLOCK)),
            _const_spec((1, ATTN_WIDTH)),
            _const_spec((1, GMLP_WIDTH)),
            _const_spec((D_MODEL, D_MODEL)),
            _const_spec((1, D_MODEL)),
        ],
        out_specs=[pl.BlockSpec((tm, D_MODEL), lag), pl.BlockSpec((tm, D_MODEL), lag)],
        out_shape=[jax.ShapeDtypeStruct((t, D_MODEL), F32),
                   jax.ShapeDtypeStruct((t, D_MODEL), BF16)],
        scratch_shapes=[
            pltpu.VMEM((tm, D_MODEL), F32),
            pltpu.VMEM((tm, D_MODEL), BF16),
            pltpu.VMEM((tm, D_MODEL), BF16),
        ],
        compiler_params=pltpu.CompilerParams(
            dimension_semantics=("arbitrary",),
            vmem_limit_bytes=V7X_VMEM_LIMIT_BYTES),
        name="mixer",
    )(x, q, kv, kv, kv, u, vg, sink, w_sp, b_sp, g_ao, g_go, w_out, g_ffn)


def _ffn_kernel(xg_ref, x_ref, wup_ref, wdn_ref, o_ref):
    f = pl.program_id(1)
    n_f = pl.num_programs(1)

    def partial_sum():
        up = jnp.dot(xg_ref[...], wup_ref[...], preferred_element_type=F32)
        act = jnp.square(jnp.maximum(up, 0.0)).astype(BF16)
        return jnp.dot(act, wdn_ref[...], preferred_element_type=F32)

    @pl.when(f == 0)
    def _():
        o_ref[...] = partial_sum()

    @pl.when((f > 0) & (f < n_f - 1))
    def _():
        o_ref[...] += partial_sum()

    @pl.when(f == n_f - 1)
    def _():
        x = x_ref[...]
        r2 = 1.0 / (jnp.mean(x * x, axis=-1, keepdims=True) + EPS)
        o_ref[...] = x + r2 * (o_ref[...] + partial_sum())


def _ffn(xg, x, w_up, w_down):
    t = x.shape[0]
    tm, tf = TM_FFN, TF_FFN
    return pl.pallas_call(
        _ffn_kernel,
        grid=(t // tm, D_FF // tf),
        in_specs=[
            pl.BlockSpec((tm, D_MODEL), lambda i, f: (i, 0)),
            pl.BlockSpec((tm, D_MODEL), lambda i, f: (i, 0)),
            pl.BlockSpec((D_MODEL, tf), lambda i, f: (0, f)),
            pl.BlockSpec((tf, D_MODEL), lambda i, f: (f, 0)),
        ],
        out_specs=pl.BlockSpec((tm, D_MODEL), lambda i, f: (i, 0)),
        out_shape=jax.ShapeDtypeStruct((t, D_MODEL), F32),
        compiler_params=pltpu.CompilerParams(
            dimension_semantics=("arbitrary", "arbitrary"),
            vmem_limit_bytes=V7X_VMEM_LIMIT_BYTES),
        name="ffn",
    )(xg, x, w_up, w_down)


def _rope_tables(seq):
    pos = np.arange(seq, dtype=np.float64)
    inv_freq = ROPE_THETA ** (-np.arange(0, ROT_DIM, 2, dtype=np.float64) / ROT_DIM)
    ang = pos[:, None] * inv_freq[None, :]
    cos, sin = np.cos(ang), np.sin(ang)
    half = ROT_DIM // 2
    pad = HEAD_DIM - ROT_DIM
    cos_t = np.concatenate([cos, cos, np.ones((seq, pad))], axis=1)
    sina_t = np.concatenate([-sin, np.zeros((seq, HEAD_DIM - half))], axis=1)
    sinb_t = np.concatenate([np.zeros((seq, half)), sin, np.zeros((seq, pad))], axis=1)
    return tuple(jnp.asarray(t, dtype=F32) for t in (cos_t, sina_t, sinb_t))


def kernel(x_prompt, x_sample, g_mix, w_in, g_q, g_k, sink, g_v_ln, b_v_ln, w_spatial, b_spatial, g_attn_out, g_gmlp_out, w_out, g_ffn, w_up, w_down):
    streams = [x_prompt, x_sample]
    for l in range(g_mix.shape[0]):
        w_in_l = w_in[l].astype(BF16)
        b_sp_l = jnp.broadcast_to(b_spatial[l][:, :, None], (N_GMLP_HEADS, BLOCK, BLOCK))
        cast_jobs = [(w_up[l], w_out[l]),
                     (w_down[l], w_spatial[l].reshape(N_GMLP_HEADS * BLOCK, BLOCK))]
        proj_out, cast_out = [], []
        for x, w_cast in zip(streams, cast_jobs):
            seq = x.shape[1]
            *qkuv, w_a, w_b = _in_proj(
                x.reshape(-1, D_MODEL), g_mix[l][None], w_in_l, g_q[l][None], g_k[l][None],
                g_v_ln[l][None], b_v_ln[l][None], _rope_tables(seq), seq, w_cast)
            proj_out.append(qkuv)
            cast_out.append((w_a, w_b))
        (w_up_l, w_out_l), (w_down_l, w_sp_l) = cast_out
        w_sp_l = w_sp_l.reshape(N_GMLP_HEADS, BLOCK, BLOCK)
        mixed = [
            _mixer(x.reshape(-1, D_MODEL), *qkuv, sink[l], w_sp_l, b_sp_l, g_attn_out[l][None],
                   g_gmlp_out[l][None], w_out_l, g_ffn[l][None], x.shape[1])
            for x, qkuv in zip(streams, proj_out)]
        streams = [
            _ffn(xg, x1, w_up_l, w_down_l).reshape(x.shape)
            for x, (x1, xg) in zip(streams, mixed)]
    return tuple(streams)
```
